```python
import functools
import jax, jax.numpy as jnp
from jax import lax
import numpy as np

D_MODEL = 1024
BATCH = 4
SEQ = 8192
DEPTH = 1
DEC_BATCH = 32
DEC_SEQ = 8
PAST_LEN = 16384
PAGE_SIZE = 128

HEAD_DIM = 64
CONV_GROUPS = 6
CONV_DIM = CONV_GROUPS * HEAD_DIM
ATT_HEADS = 6
ATT_DIM = ATT_HEADS * HEAD_DIM
MEM_HEADS = 4
MEM_DIM = MEM_HEADS * HEAD_DIM
MIX_DIM = CONV_DIM + ATT_DIM + MEM_DIM
IN_DIM = 3 * CONV_DIM + 3 * ATT_DIM + MEM_DIM
SPLITS = (CONV_DIM, 2 * CONV_DIM, 3 * CONV_DIM, 3 * CONV_DIM + ATT_DIM,
          3 * CONV_DIM + 2 * ATT_DIM, 3 * CONV_DIM + 3 * ATT_DIM)
MEM_LEN = 256
CONV_W = 3
MOBA_BLOCK = 256
MOBA_TOPK = 3
Q_BLOCK = 128
D_FF = 2816
EPS = 1e-6

kernel_name = 'hymba_conv_moba_mem_convffn_step'


def rmsnorm(x, g):
    xf = x.astype(jnp.float32)
    xf = xf * lax.rsqrt(jnp.mean(xf * xf, axis=-1, keepdims=True) + EPS)
    return (xf * g.astype(jnp.float32)).astype(x.dtype)


def alibi_slopes(n):
    return jnp.exp2(-8.0 * jnp.arange(1, n + 1, dtype=jnp.float32) / n)


def causal_conv(u, prev, w):
    T = u.shape[1]
    ext = jnp.concatenate([prev.astype(u.dtype), u], axis=1)
    y = w[0] * ext[:, 0:T]
    for j in range(1, CONV_W):
        y = y + w[j] * ext[:, j:j + T]
    return y, ext[:, T:]


def moba_attend(q, t_pos, k_own, v_own, pos_own, slopes, k_sel=None, v_sel=None, pos_sel=None, ok_sel=None):
    scale = HEAD_DIM ** -0.5
    qf = q.astype(jnp.float32)
    s_own = jnp.einsum('bthd,blhd->bthl', qf, k_own.astype(jnp.float32)) * scale
    d_own = (t_pos[:, None] - pos_own[None, :]).astype(jnp.float32)
    s_own = s_own - slopes[None, None, :, None] * d_own[None, :, None, :]
    s_own = jnp.where((d_own >= 0)[None, :, None, :], s_own, -jnp.inf)
    if k_sel is None:
        p = jax.nn.softmax(s_own, axis=-1)
        out = jnp.einsum('bthl,blhd->bthd', p, v_own.astype(jnp.float32))
        return out.astype(q.dtype)
    L = k_own.shape[1]
    s_sel = jnp.einsum('bthd,bthmd->bthm', qf, k_sel.astype(jnp.float32)) * scale
    d_sel = (t_pos[None, :, None, None] - pos_sel).astype(jnp.float32)
    s_sel = s_sel - slopes[None, None, :, None] * d_sel
    s_sel = jnp.where(ok_sel, s_sel, -jnp.inf)
    p = jax.nn.softmax(jnp.concatenate([s_own, s_sel], axis=-1), axis=-1)
    out = (jnp.einsum('bthl,blhd->bthd', p[..., :L], v_own.astype(jnp.float32))
           + jnp.einsum('bthm,bthmd->bthd', p[..., L:], v_sel.astype(jnp.float32)))
    return out.astype(q.dtype)


def moba_prompt(q, k, v, slopes):
    nbat, S = q.shape[0], q.shape[1]
    nb = -(-S // MOBA_BLOCK)
    pad = nb * MOBA_BLOCK - S
    kb = jnp.pad(k, ((0, 0), (0, pad), (0, 0), (0, 0))).reshape(nbat, nb, MOBA_BLOCK, ATT_HEADS, HEAD_DIM)
    vb = jnp.pad(v, ((0, 0), (0, pad), (0, 0), (0, 0))).reshape(nbat, nb, MOBA_BLOCK, ATT_HEADS, HEAD_DIM)
    kk = min(MOBA_TOPK, nb - 1)
    nq = S // Q_BLOCK
    ids = jnp.arange(nbat * nq, dtype=jnp.int32)
    xs = (ids // nq, ids % nq, q.reshape(nbat * nq, Q_BLOCK, ATT_HEADS, HEAD_DIM))
    if kk > 0:
        kmean = kb.astype(jnp.float32).mean(axis=2)
        gate = jnp.einsum('bshd,bnhd->bshn', q.astype(jnp.float32), kmean)
        t_blk = jnp.arange(S, dtype=jnp.int32) // MOBA_BLOCK
        past = jnp.arange(nb, dtype=jnp.int32)[None, :] < t_blk[:, None]
        gate = jnp.where(past[None, :, None, :], gate, -jnp.inf)
        gval, sel = lax.top_k(gate, kk)
        xs = xs + (sel.reshape(nbat * nq, Q_BLOCK, ATT_HEADS, kk),
                   jnp.isfinite(gval).reshape(nbat * nq, Q_BLOCK, ATT_HEADS, kk))
    blk_pos = jnp.arange(MOBA_BLOCK, dtype=jnp.int32)
    head_ix = jnp.arange(ATT_HEADS).reshape(ATT_HEADS, 1, 1)
    m = kk * MOBA_BLOCK

    def one(args):
        b, qi, q_blk = args[0], args[1], args[2]
        t_pos = qi * Q_BLOCK + jnp.arange(Q_BLOCK, dtype=jnp.int32)
        ob = (qi * Q_BLOCK) // MOBA_BLOCK
        k_own = kb[b, ob][None]
        v_own = vb[b, ob][None]
        pos_own = ob * MOBA_BLOCK + blk_pos
        if kk == 0:
            return moba_attend(q_blk[None], t_pos, k_own, v_own, pos_own, slopes)[0]
        idx = args[3][..., None]
        k_sel = kb[b, idx, blk_pos, head_ix].reshape(1, Q_BLOCK, ATT_HEADS, m, HEAD_DIM)
        v_sel = vb[b, idx, blk_pos, head_ix].reshape(1, Q_BLOCK, ATT_HEADS, m, HEAD_DIM)
        pos_sel = (idx * MOBA_BLOCK + blk_pos).reshape(1, Q_BLOCK, ATT_HEADS, m)
        ok_sel = jnp.broadcast_to(args[4][..., None], (Q_BLOCK, ATT_HEADS, kk, MOBA_BLOCK)).reshape(1, Q_BLOCK, ATT_HEADS, m)
        return moba_attend(q_blk[None], t_pos, k_own, v_own, pos_own, slopes, k_sel, v_sel, pos_sel, ok_sel)[0]

    out = lax.map(one, xs)
    return out.reshape(nbat, S, ATT_HEADS, HEAD_DIM)


def moba_sample(q, k_new, v_new, slopes, cache_k, cache_v, page_table):
    nbat, T = q.shape[0], q.shape[1]
    ppb = MOBA_BLOCK // PAGE_SIZE
    nfp = PAST_LEN // MOBA_BLOCK
    cb = nfp * MOBA_BLOCK
    kk = min(MOBA_TOPK, nfp)
    t_pos = PAST_LEN + jnp.arange(T, dtype=jnp.int32)
    n_own = (PAST_LEN - cb) // PAGE_SIZE
    own_pages = page_table[:, cb // PAGE_SIZE: cb // PAGE_SIZE + n_own]
    k_past = cache_k[own_pages].reshape(nbat, n_own * PAGE_SIZE, ATT_HEADS, HEAD_DIM)
    v_past = cache_v[own_pages].reshape(nbat, n_own * PAGE_SIZE, ATT_HEADS, HEAD_DIM)
    k_own = jnp.concatenate([k_past.astype(k_new.dtype), k_new], axis=1)
    v_own = jnp.concatenate([v_past.astype(v_new.dtype), v_new], axis=1)
    pos_own = jnp.concatenate([jnp.arange(cb, PAST_LEN, dtype=jnp.int32), t_pos])
    if kk == 0:
        return moba_attend(q, t_pos, k_own, v_own, pos_own, slopes)
    page_sum = cache_k.astype(jnp.float32).sum(axis=1)
    kmean = page_sum[page_table[:, :nfp * ppb]].reshape(nbat, nfp, ppb, ATT_HEADS, HEAD_DIM).sum(axis=2) / MOBA_BLOCK
    gate = jnp.einsum('bthd,bnhd->bthn', q.astype(jnp.float32), kmean)
    _, sel = lax.top_k(gate, kk)
    lpage = sel[..., None] * ppb + jnp.arange(ppb, dtype=jnp.int32)
    phys = page_table[jnp.arange(nbat).reshape(nbat, 1, 1, 1, 1), lpage]
    row_ix = jnp.arange(PAGE_SIZE)
    head_ix = jnp.arange(ATT_HEADS).reshape(ATT_HEADS, 1, 1, 1)
    m = kk * MOBA_BLOCK
    k_sel = cache_k[phys[..., None], row_ix, head_ix].reshape(nbat, T, ATT_HEADS, m, HEAD_DIM)
    v_sel = cache_v[phys[..., None], row_ix, head_ix].reshape(nbat, T, ATT_HEADS, m, HEAD_DIM)
    pos_sel = (sel[..., None] * MOBA_BLOCK + jnp.arange(MOBA_BLOCK, dtype=jnp.int32)).reshape(nbat, T, ATT_HEADS, m)
    ok_sel = jnp.ones(pos_sel.shape, dtype=bool)
    return moba_attend(q, t_pos, k_own, v_own, pos_own, slopes, k_sel, v_sel, pos_sel, ok_sel)


def memory_kv(mem, mem_norm, w_mem_kv, memk_norm):
    nbat = mem.shape[0]
    kv = rmsnorm(mem, mem_norm) @ w_mem_kv
    mk, mv = jnp.split(kv, 2, axis=-1)
    mk = rmsnorm(mk.reshape(nbat, MEM_LEN, MEM_HEADS, HEAD_DIM), memk_norm)
    return mk, mv.reshape(nbat, MEM_LEN, MEM_HEADS, HEAD_DIM)


def memory_attend(q, mem_k, mem_v):
    s = jnp.einsum('bthd,bmhd->bthm', q.astype(jnp.float32), mem_k.astype(jnp.float32)) * (HEAD_DIM ** -0.5)
    p = jax.nn.softmax(s, axis=-1)
    return jnp.einsum('bthm,bmhd->bthd', p, mem_v.astype(jnp.float32)).astype(q.dtype)


def block_forward(x, conv_prev, ffn_prev, mem_k, mem_v, moba_fn, norm_mix, w_in, conv_w, q_norm, k_norm,
                  memq_norm, w_o, norm_ffn, w_up, ffn_conv_w, w_down):
    nbat, T = x.shape[0], x.shape[1]
    proj = rmsnorm(x, norm_mix) @ w_in
    c_b, c_c, c_h, q, k, v, mq = jnp.split(proj, SPLITS, axis=-1)
    u_conv, conv_state = causal_conv(c_c * c_h, conv_prev, conv_w)
    y_conv = c_b * u_conv
    q = rmsnorm(q.reshape(nbat, T, ATT_HEADS, HEAD_DIM), q_norm)
    k = rmsnorm(k.reshape(nbat, T, ATT_HEADS, HEAD_DIM), k_norm)
    v = v.reshape(nbat, T, ATT_HEADS, HEAD_DIM)
    y_att = moba_fn(q, k, v).reshape(nbat, T, ATT_DIM)
    mq = rmsnorm(mq.reshape(nbat, T, MEM_HEADS, HEAD_DIM), memq_norm)
    y_mem = memory_attend(mq, mem_k, mem_v).reshape(nbat, T, MEM_DIM)
    h = x + jnp.concatenate([y_conv, y_att, y_mem], axis=-1) @ w_o
    up = rmsnorm(h, norm_ffn) @ w_up
    up_conv, ffn_state = causal_conv(up, ffn_prev, ffn_conv_w)
    a, g = jnp.split(up_conv, 2, axis=-1)
    y = h + (jax.nn.silu(g) * a) @ w_down
    return y, k, v, conv_state, ffn_state


def setup_inputs(seed: int = 0) -> dict:
    key = jax.random.key(seed)
    ks = jax.random.split(key, 32)
    f32 = jnp.float32
    n_pages = PAST_LEN // PAGE_SIZE
    n_used = DEC_BATCH * n_pages
    n_pool = (5 * n_used + 3) // 4

    def nrm(k, shape, scale=1.0):
        return jax.random.normal(k, shape, f32) * scale

    def gain(k, shape):
        return 1.0 + 0.05 * jax.random.normal(k, shape, f32)

    page_table = jax.random.permutation(ks[0], n_pool)[:n_used].reshape(DEC_BATCH, n_pages).astype(jnp.int32)
    return {
        'x_prompt': nrm(ks[1], (BATCH, SEQ, D_MODEL)),
        'x_sample': nrm(ks[2], (DEC_BATCH, DEC_SEQ, D_MODEL)),
        'mem_prompt': nrm(ks[3], (BATCH, MEM_LEN, D_MODEL)),
        'cache_k': nrm(ks[4], (DEPTH, n_pool, PAGE_SIZE, ATT_HEADS, HEAD_DIM)),
        'cache_v': nrm(ks[5], (DEPTH, n_pool, PAGE_SIZE, ATT_HEADS, HEAD_DIM)),
        'page_table': page_table,
        'cache_conv': nrm(ks[6], (DEPTH, DEC_BATCH, CONV_W - 1, CONV_DIM)),
        'cache_ffn_conv': nrm(ks[7], (DEPTH, DEC_BATCH, CONV_W - 1, 2 * D_FF)),
        'cache_mem_k': nrm(ks[8], (DEPTH, DEC_BATCH, MEM_LEN, MEM_HEADS, HEAD_DIM)),
        'cache_mem_v': nrm(ks[9], (DEPTH, DEC_BATCH, MEM_LEN, MEM_HEADS, HEAD_DIM)),
        'norm_mix': gain(ks[10], (DEPTH, D_MODEL)),
        'w_in': nrm(ks[11], (DEPTH, D_MODEL, IN_DIM), D_MODEL ** -0.5),
        'conv_w': nrm(ks[12], (DEPTH, CONV_W, CONV_DIM), CONV_W ** -0.5),
        'q_norm': gain(ks[13], (DEPTH, HEAD_DIM)),
        'k_norm': gain(ks[14], (DEPTH, HEAD_DIM)),
        'memq_norm': gain(ks[15], (DEPTH, HEAD_DIM)),
        'mem_norm': gain(ks[16], (DEPTH, D_MODEL)),
        'w_mem_kv': nrm(ks[17], (DEPTH, D_MODEL, 2 * MEM_DIM), D_MODEL ** -0.5),
        'memk_norm': gain(ks[18], (DEPTH, HEAD_DIM)),
        'w_o': nrm(ks[19], (DEPTH, MIX_DIM, D_MODEL), MIX_DIM ** -0.5),
        'norm_ffn': gain(ks[20], (DEPTH, D_MODEL)),
        'w_up': nrm(ks[21], (DEPTH, D_MODEL, 2 * D_FF), D_MODEL ** -0.5),
        'ffn_conv_w': nrm(ks[22], (DEPTH, CONV_W, 2 * D_FF), CONV_W ** -0.5),
        'w_down': nrm(ks[23], (DEPTH, D_FF, D_MODEL), D_FF ** -0.5),
    }


def reference(x_prompt, x_sample, mem_prompt, cache_k, cache_v, page_table, cache_conv, cache_ffn_conv,
              cache_mem_k, cache_mem_v, norm_mix, w_in, conv_w, q_norm, k_norm, memq_norm, mem_norm,
              w_mem_kv, memk_norm, w_o, norm_ffn, w_up, ffn_conv_w, w_down):
    slopes = alibi_slopes(ATT_HEADS)
    yp, ys = x_prompt, x_sample
    nbp = x_prompt.shape[0]
    kp_l, vp_l, ks_l, vs_l, cp_l, cs_l, fp_l, fs_l, mkp_l, mvp_l = [], [], [], [], [], [], [], [], [], []
    for l in range(DEPTH):
        lw = (norm_mix[l], w_in[l], conv_w[l], q_norm[l], k_norm[l], memq_norm[l], w_o[l], norm_ffn[l],
              w_up[l], ffn_conv_w[l], w_down[l])
        mk_p, mv_p = memory_kv(mem_prompt, mem_norm[l], w_mem_kv[l], memk_norm[l])
        conv0 = jnp.zeros((nbp, CONV_W - 1, CONV_DIM), x_prompt.dtype)
        ffn0 = jnp.zeros((nbp, CONV_W - 1, 2 * D_FF), x_prompt.dtype)
        yp, kp, vp, cp, fp = block_forward(yp, conv0, ffn0, mk_p, mv_p,
                                           functools.partial(moba_prompt, slopes=slopes), *lw)
        ys, k_s, v_s, c_s, f_s = block_forward(
            ys, cache_conv[l], cache_ffn_conv[l], cache_mem_k[l], cache_mem_v[l],
            functools.partial(moba_sample, slopes=slopes, cache_k=cache_k[l], cache_v=cache_v[l],
                              page_table=page_table), *lw)
        kp_l.append(kp); vp_l.append(vp); ks_l.append(k_s); vs_l.append(v_s)
        cp_l.append(cp); cs_l.append(c_s); fp_l.append(fp); fs_l.append(f_s)
        mkp_l.append(mk_p); mvp_l.append(mv_p)
    return (yp, ys, jnp.stack(kp_l), jnp.stack(vp_l), jnp.stack(ks_l), jnp.stack(vs_l),
            jnp.stack(cp_l), jnp.stack(cs_l), jnp.stack(fp_l), jnp.stack(fs_l),
            jnp.stack(mkp_l), jnp.stack(mvp_l))
```

```python
import functools

import jax
import jax.numpy as jnp
from jax import lax
from jax.experimental import pallas as pl
from jax.experimental.pallas import tpu as pltpu

HEAD_DIM = 64
CONV_DIM = 384
ATT_HEADS = 6
ATT_DIM = ATT_HEADS * HEAD_DIM
MEM_HEADS = 4
MEM_DIM = MEM_HEADS * HEAD_DIM
MOBA_BLOCK = 256
MOBA_TOPK = 3
PAGE_SIZE = 128
PAGES_PER_BLOCK = MOBA_BLOCK // PAGE_SIZE
EPS = 1e-6
SCALE = HEAD_DIM ** -0.5

LANES = 128
SUBLANES = 8
NEG = -1e30
SEL_LANE0 = HEAD_DIM
ALIBI_LANE0 = SEL_LANE0 + 32
MAX_BLOCKS = ALIBI_LANE0 - SEL_LANE0
FFN_CHUNK = 256
VMEM_LIMIT = 48 * 1024 * 1024

F32 = jnp.float32
BF16 = jnp.bfloat16


def _nt(a, b):
    return lax.dot_general(a, b, (((1,), (1,)), ((), ())), preferred_element_type=F32)


def _mm(a, b):
    return jnp.dot(a, b, preferred_element_type=F32)


def _split(x):
    hi = x.astype(BF16)
    lo = (x - hi.astype(F32)).astype(BF16)
    return hi, lo


def _nt3(a, b):
    a_hi, a_lo = _split(a)
    b_hi, b_lo = _split(b)
    return _nt(a_hi, b_hi) + _nt(a_hi, b_lo) + _nt(a_lo, b_hi)


def _rms(x, g):
    return x * lax.rsqrt(jnp.mean(x * x, axis=-1, keepdims=True) + EPS) * g


def _head_rms(x, bd, g):
    hi, lo = _split(x * x)
    ms = _mm(hi, bd) + _mm(lo, bd)
    return x * lax.rsqrt(ms + EPS) * g


def _conv3(u, u1, u2, w):
    return w[0:1] * u2 + w[1:2] * u1 + w[2:3] * u


def _shift_tile(u, prev8):
    row = lax.broadcasted_iota(jnp.int32, u.shape, 0)
    p6 = jnp.broadcast_to(prev8[6:7], u.shape)
    p7 = jnp.broadcast_to(prev8[7:8], u.shape)
    u1 = jnp.where(row == 0, p7, pltpu.roll(u, 1, 0))
    u2 = jnp.where(row == 0, p6, jnp.where(row == 1, p7, pltpu.roll(u, 2, 0)))
    return u1, u2


def _shift_groups(u, prev_ext):
    n = u.shape[0]
    r8 = lax.broadcasted_iota(jnp.int32, u.shape, 0) % SUBLANES
    u1 = jnp.where(r8 == 0, pltpu.roll(prev_ext, n - 1, 0), pltpu.roll(u, 1, 0))
    u2 = jnp.where(r8 < 2, prev_ext, pltpu.roll(u, 2, 0))
    return u1, u2


def _project(x, nm_ref, w_ref, cw_ref, qg_ref, kg_ref, mqg_ref, bd_ref, shift):
    xn = _rms(x, nm_ref[...]).astype(BF16)
    seg = lambda a, b: _mm(xn, w_ref[:, a:b])
    c0, c1, c2 = CONV_DIM, 2 * CONV_DIM, 3 * CONV_DIM
    a1, a2, a3 = c2 + ATT_DIM, c2 + 2 * ATT_DIM, c2 + 3 * ATT_DIM
    u = seg(c0, c1) * seg(c1, c2)
    u1, u2 = shift(u)
    y_conv = seg(0, c0) * _conv3(u, u1, u2, cw_ref[...])
    bd = bd_ref[...]
    q = _head_rms(seg(c2, a1), bd, qg_ref[...])
    k = _head_rms(seg(a1, a2), bd, kg_ref[...])
    v = seg(a2, a3)
    mq = _head_rms(seg(a3, a3 + MEM_DIM), bd[:MEM_DIM, :MEM_DIM], mqg_ref[...])
    return u, y_conv, q, k, v, mq


def _inproj_prompt_kernel(slopes_ref, x_ref, nm_ref, w_ref, cw_ref, qg_ref, kg_ref, mqg_ref, bd_ref,
                          yc_ref, k_ref, v_ref, mq_ref, qa_ref, ka_ref, va_ref, cs_ref,
                          carry_ref, km_ref):
    i = pl.program_id(1)
    tm = x_ref.shape[1]

    @pl.when(i == 0)
    def _():
        carry_ref[...] = jnp.zeros_like(carry_ref)
        km_ref[...] = jnp.zeros_like(km_ref)

    shift = lambda u: _shift_tile(u, carry_ref[...])
    u, y_conv, q, k, v, mq = _project(x_ref[0], nm_ref, w_ref, cw_ref, qg_ref, kg_ref, mqg_ref, bd_ref, shift)
    carry_ref[...] = u[tm - SUBLANES:]
    cs_ref[0] = u[tm - SUBLANES:]
    yc_ref[0] = y_conv.astype(BF16)
    k_ref[0] = k
    v_ref[0] = v
    mq_ref[0] = (mq * SCALE).astype(BF16)

    gate = _nt3(q, km_ref[...])
    lane = lax.broadcasted_iota(jnp.int32, (tm, LANES), 1)
    lane_f = lane.astype(F32)
    row = lax.broadcasted_iota(jnp.int32, (tm, LANES), 0)
    pos = (i * tm + row).astype(F32)
    own = lane == SEL_LANE0 + i
    past = (lane >= SEL_LANE0) & (lane < SEL_LANE0 + i)
    for h in range(ATT_HEADS):
        g = jnp.where(past, gate[:, h * LANES:(h + 1) * LANES], -jnp.inf)
        sel = own
        for _ in range(MOBA_TOPK):
            m = jnp.max(g, axis=-1, keepdims=True)
            first = jnp.min(jnp.where((g == m) & past, lane_f, float(LANES)), axis=-1, keepdims=True)
            pick = (lane_f == first) & (m > -jnp.inf)
            sel = sel | pick
            g = jnp.where(pick, -jnp.inf, g)
        sel_bias = jnp.where(sel, 0.0, NEG)

        pair = slice((h // 2) * LANES, (h // 2 + 1) * LANES)
        head = (lambda t: t[:, pair]) if h % 2 == 0 else (lambda t: pltpu.roll(t[:, pair], HEAD_DIM, 1))
        low = lane < HEAD_DIM
        q_aux = jnp.where(lane < ALIBI_LANE0, sel_bias, jnp.where(lane < ALIBI_LANE0 + 3, 1.0, 0.0))
        qa_ref[0, h] = jnp.where(low, head(q) * SCALE, q_aux).astype(BF16)

        a = slopes_ref[h] * pos
        a_hi = a.astype(BF16).astype(F32)
        a_mid = (a - a_hi).astype(BF16).astype(F32)
        a_lo = a - a_hi - a_mid
        k_aux = jnp.where(own, 1.0, 0.0)
        k_aux = jnp.where(lane == ALIBI_LANE0, a_hi, k_aux)
        k_aux = jnp.where(lane == ALIBI_LANE0 + 1, a_mid, k_aux)
        k_aux = jnp.where(lane == ALIBI_LANE0 + 2, a_lo, k_aux)
        ka_ref[0, h] = jnp.where(low, head(k), k_aux).astype(BF16)
        va_ref[0, h] = jnp.where(low, head(v), jnp.where(lane == HEAD_DIM, 1.0, 0.0)).astype(BF16)

    kmean = jnp.sum(k, axis=0, keepdims=True) * (1.0 / MOBA_BLOCK)
    head_of_lane = lax.broadcasted_iota(jnp.int32, kmean.shape, 1) // HEAD_DIM
    for h in range(ATT_HEADS):
        km_ref[pl.ds(h * LANES + SEL_LANE0 + i, 1), :] = jnp.where(head_of_lane == h, kmean, 0.0)


def _inproj_prompt(slopes, x, nm, w_in, cw, qg, kg, mqg, bd):
    nb, s, d = x.shape
    tm = MOBA_BLOCK
    nt = s // tm
    assert s % tm == 0 and nt <= MAX_BLOCKS
    const = lambda shape: pl.BlockSpec(shape, lambda b, i: (0,) * len(shape), pipeline_mode=pl.Buffered(1))
    tok = lambda w: pl.BlockSpec((1, tm, w), lambda b, i: (b, i, 0))
    aug = pl.BlockSpec((1, ATT_HEADS, tm, LANES), lambda b, i: (b, 0, i, 0))
    aug_shape = jax.ShapeDtypeStruct((nb, ATT_HEADS, s, LANES), BF16)
    return pl.pallas_call(
        _inproj_prompt_kernel,
        grid=(nb, nt),
        in_specs=[pl.BlockSpec(memory_space=pltpu.SMEM), tok(d), const(nm.shape), const(w_in.shape),
                  const(cw.shape), const(qg.shape), const(kg.shape), const(mqg.shape), const(bd.shape)],
        out_specs=[tok(CONV_DIM), tok(ATT_DIM), tok(ATT_DIM), tok(MEM_DIM), aug, aug, aug,
                   pl.BlockSpec((1, SUBLANES, CONV_DIM), lambda b, i: (b, 0, 0))],
        out_shape=[jax.ShapeDtypeStruct((nb, s, CONV_DIM), BF16),
                   jax.ShapeDtypeStruct((nb, s, ATT_DIM), F32),
                   jax.ShapeDtypeStruct((nb, s, ATT_DIM), F32),
                   jax.ShapeDtypeStruct((nb, s, MEM_DIM), BF16),
                   aug_shape, aug_shape, aug_shape,
                   jax.ShapeDtypeStruct((nb, SUBLANES, CONV_DIM), F32)],
        scratch_shapes=[pltpu.VMEM((SUBLANES, CONV_DIM), F32),
                        pltpu.VMEM((ATT_HEADS * LANES, ATT_DIM), F32)],
        compiler_params=pltpu.CompilerParams(dimension_semantics=("arbitrary", "arbitrary"),
                                             vmem_limit_bytes=VMEM_LIMIT),
        name="inproj_prompt",
    )(slopes, x, nm, w_in, cw, qg, kg, mqg, bd)


def _moba_prompt_kernel(qa_ref, ka_ref, va_ref, o_ref):
    i = pl.program_id(2)
    tq = qa_ref.shape[2]
    causal = (lax.broadcasted_iota(jnp.int32, (tq, tq), 0) >= lax.broadcasted_iota(jnp.int32, (tq, tq), 1))
    outs = []
    for hh in range(2):
        q = qa_ref[0, hh]

        def tile(j, hh=hh, q=q):
            rows = pl.ds(pl.multiple_of(j * tq, tq), tq)
            return _nt(q, ka_ref[0, hh, rows, :]), va_ref[0, hh, rows, :]

        s, vj = tile(i)
        s = jnp.where(causal, s, NEG)
        m = jnp.max(s, axis=-1, keepdims=True)
        acc = _mm(jnp.exp(s - m).astype(BF16), vj)

        def body(j, carry, tile=tile):
            m, acc = carry
            s, vj = tile(j)
            m_new = jnp.maximum(m, jnp.max(s, axis=-1, keepdims=True))
            acc = jnp.exp(m - m_new) * acc + _mm(jnp.exp(s - m_new).astype(BF16), vj)
            return m_new, acc

        m, acc = lax.fori_loop(0, i, body, (m, acc))
        outs.append(acc * (1.0 / acc[:, HEAD_DIM:HEAD_DIM + 1]))
    lane = lax.broadcasted_iota(jnp.int32, (tq, LANES), 1)
    o_ref[0] = jnp.where(lane < HEAD_DIM, outs[0], pltpu.roll(outs[1], HEAD_DIM, 1)).astype(BF16)


def _moba_prompt(q_aug, k_aug, v_aug):
    nb, _, s, _ = q_aug.shape
    tq = MOBA_BLOCK
    kv = pl.BlockSpec((1, 2, s, LANES), lambda b, p, i: (b, p, 0, 0))
    return pl.pallas_call(
        _moba_prompt_kernel,
        grid=(nb, ATT_HEADS // 2, s // tq),
        in_specs=[pl.BlockSpec((1, 2, tq, LANES), lambda b, p, i: (b, p, i, 0)), kv, kv],
        out_specs=pl.BlockSpec((1, tq, LANES), lambda b, p, i: (b, i, p)),
        out_shape=jax.ShapeDtypeStruct((nb, s, ATT_DIM), BF16),
        compiler_params=pltpu.CompilerParams(dimension_semantics=("arbitrary", "arbitrary", "arbitrary"),
                                             vmem_limit_bytes=VMEM_LIMIT),
        name="moba_prompt",
    )(q_aug, k_aug, v_aug)


def _memkv_kernel(mem_ref, g_ref, w_ref, kg_ref, bd_ref, mk_ref, mv_ref):
    kv = _mm(_rms(mem_ref[0], g_ref[...]).astype(BF16), w_ref[...])
    mk = _head_rms(kv[:, :MEM_DIM], bd_ref[...], kg_ref[...])
    mk_ref[0] = mk.T
    mv_ref[0] = kv[:, MEM_DIM:].T


def _memkv(mem, g, w, kg, bd):
    nb, m, d = mem.shape
    const = lambda a: pl.BlockSpec(a.shape, lambda b: (0,) * a.ndim)
    out = pl.BlockSpec((1, MEM_DIM, m), lambda b: (b, 0, 0))
    return pl.pallas_call(
        _memkv_kernel,
        grid=(nb,),
        in_specs=[pl.BlockSpec((1, m, d), lambda b: (b, 0, 0)), const(g), const(w), const(kg), const(bd)],
        out_specs=[out, out],
        out_shape=[jax.ShapeDtypeStruct((nb, MEM_DIM, m), F32)] * 2,
        compiler_params=pltpu.CompilerParams(dimension_semantics=("arbitrary",)),
        name="memory_kv",
    )(mem, g, w, kg, bd)


def _memattn_kernel(mq_ref, mk_ref, mv_ref, o_ref):
    mq = mq_ref[0]
    mk = mk_ref[0].astype(BF16)
    mv = mv_ref[0].astype(BF16)
    head_of_lane = lax.broadcasted_iota(jnp.int32, mq.shape, 1) // HEAD_DIM
    out = jnp.zeros(mq.shape, F32)
    for h in range(MEM_HEADS):
        mine = head_of_lane == h
        s = _mm(jnp.where(mine, mq, jnp.zeros_like(mq)), mk)
        p = jnp.exp(s - jnp.max(s, axis=-1, keepdims=True))
        inv = 1.0 / jnp.sum(p, axis=-1, keepdims=True)
        out = jnp.where(mine, _nt(p.astype(BF16), mv) * inv, out)
    o_ref[0] = out.astype(BF16)


def _memattn(mq, mk_t, mv_t, tq):
    nb, t, _ = mq.shape
    m = mk_t.shape[-1]
    kv = pl.BlockSpec((1, MEM_DIM, m), lambda b, i: (b, 0, 0))
    return pl.pallas_call(
        _memattn_kernel,
        grid=(nb, t // tq),
        in_specs=[pl.BlockSpec((1, tq, MEM_DIM), lambda b, i: (b, i, 0)), kv, kv],
        out_specs=pl.BlockSpec((1, tq, MEM_DIM), lambda b, i: (b, i, 0)),
        out_shape=jax.ShapeDtypeStruct((nb, t, MEM_DIM), BF16),
        compiler_params=pltpu.CompilerParams(dimension_semantics=("arbitrary", "arbitrary")),
        name="memory_attention",
    )(mq, mk_t, mv_t)


def _post(x, yc, ya, ym, wo_ref, nf_ref, wu_ref, fcw_ref, wd_ref, act_ref, shift, keep):
    h = (x + _mm(yc, wo_ref[0:CONV_DIM]) + _mm(ya, wo_ref[CONV_DIM:CONV_DIM + ATT_DIM])
         + _mm(ym, wo_ref[CONV_DIM + ATT_DIM:]))
    hn = _rms(h, nf_ref[...]).astype(BF16)
    d_ff = wd_ref.shape[0]
    for c in range(d_ff // FFN_CHUNK):
        halves = []
        for base in (0, d_ff):
            cols = slice(base + c * FFN_CHUNK, base + (c + 1) * FFN_CHUNK)
            up = _mm(hn, wu_ref[:, cols])
            u1, u2 = shift(up, cols)
            keep(up, cols)
            halves.append(_conv3(up, u1, u2, fcw_ref[:, cols]))
        a, g = halves
        act = g * (1.0 / (1.0 + jnp.exp(-g))) * a
        act_ref[:, c * FFN_CHUNK:(c + 1) * FFN_CHUNK] = act.astype(BF16)
    return h + _mm(act_ref[...], wd_ref[...])


def _post_prompt_kernel(x_ref, yc_ref, ya_ref, ym_ref, wo_ref, nf_ref, wu_ref, fcw_ref, wd_ref,
                        y_ref, fs_ref, carry_ref, act_ref):
    tm = x_ref.shape[1]

    @pl.when(pl.program_id(1) == 0)
    def _():
        carry_ref[...] = jnp.zeros_like(carry_ref)

    def keep(up, cols):
        carry_ref[:, cols] = up[tm - SUBLANES:]
        fs_ref[0, :, cols] = up[tm - SUBLANES:]

    shift = lambda up, cols: _shift_tile(up, carry_ref[:, cols])
    y_ref[0] = _post(x_ref[0], yc_ref[0], ya_ref[0], ym_ref[0], wo_ref, nf_ref, wu_ref, fcw_ref, wd_ref,
                     act_ref, shift, keep)


def _post_prompt(x, yc, ya, ym, w_o, nf, w_up, fcw, w_down):
    nb, s, d = x.shape
    tm = MOBA_BLOCK
    f2 = w_up.shape[1]
    const = lambda a: pl.BlockSpec(a.shape, lambda b, i: (0,) * a.ndim, pipeline_mode=pl.Buffered(1))
    tok = lambda w: pl.BlockSpec((1, tm, w), lambda b, i: (b, i, 0))
    return pl.pallas_call(
        _post_prompt_kernel,
        grid=(nb, s // tm),
        in_specs=[tok(d), tok(CONV_DIM), tok(ATT_DIM), tok(MEM_DIM),
                  const(w_o), const(nf), const(w_up), const(fcw), const(w_down)],
        out_specs=[tok(d), pl.BlockSpec((1, SUBLANES, f2), lambda b, i: (b, 0, 0))],
        out_shape=[jax.ShapeDtypeStruct((nb, s, d), F32), jax.ShapeDtypeStruct((nb, SUBLANES, f2), F32)],
        scratch_shapes=[pltpu.VMEM((SUBLANES, f2), F32), pltpu.VMEM((tm, f2 // 2), BF16)],
        compiler_params=pltpu.CompilerParams(dimension_semantics=("arbitrary", "arbitrary"),
                                             vmem_limit_bytes=VMEM_LIMIT),
        name="post_prompt",
    )(x, yc, ya, ym, w_o, nf, w_up, fcw, w_down)


def _inproj_sample_kernel(x_ref, pe_ref, nm_ref, w_ref, cw_ref, qg_ref, kg_ref, mqg_ref, bd_ref,
                          u_ref, yc_ref, q_ref, k_ref, v_ref, mq_ref):
    shift = lambda u: _shift_groups(u, pe_ref[...])
    u, y_conv, q, k, v, mq = _project(x_ref[...], nm_ref, w_ref, cw_ref, qg_ref, kg_ref, mqg_ref, bd_ref, shift)
    u_ref[...] = u
    yc_ref[...] = y_conv.astype(BF16)
    q_ref[...] = q
    k_ref[...] = k
    v_ref[...] = v
    mq_ref[...] = (mq * SCALE).astype(BF16)


def _inproj_sample(x, prev_ext, nm, w_in, cw, qg, kg, mqg, bd):
    n = x.shape[0]
    shp = lambda w, dt: jax.ShapeDtypeStruct((n, w), dt)
    return pl.pallas_call(
        _inproj_sample_kernel,
        out_shape=[shp(CONV_DIM, F32), shp(CONV_DIM, BF16), shp(ATT_DIM, F32), shp(ATT_DIM, F32),
                   shp(ATT_DIM, F32), shp(MEM_DIM, BF16)],
        compiler_params=pltpu.CompilerParams(vmem_limit_bytes=VMEM_LIMIT),
        name="inproj_sample",
    )(x, prev_ext, nm, w_in, cw, qg, kg, mqg, bd)


def _post_sample_kernel(x_ref, yc_ref, ya_ref, ym_ref, pe_ref, wo_ref, nf_ref, wu_ref, fcw_ref, wd_ref,
                        y_ref, up_ref, act_ref):
    def keep(up, cols):
        up_ref[:, cols] = up

    shift = lambda up, cols: _shift_groups(up, pe_ref[:, cols])
    y_ref[...] = _post(x_ref[...], yc_ref[...], ya_ref[...], ym_ref[...], wo_ref, nf_ref, wu_ref, fcw_ref,
                       wd_ref, act_ref, shift, keep)


def _post_sample(x, yc, ya, ym, prev_ext, w_o, nf, w_up, fcw, w_down):
    n, d = x.shape
    return pl.pallas_call(
        _post_sample_kernel,
        out_shape=[jax.ShapeDtypeStruct((n, d), F32), jax.ShapeDtypeStruct((n, w_up.shape[1]), F32)],
        scratch_shapes=[pltpu.VMEM((n, w_down.shape[0]), BF16)],
        compiler_params=pltpu.CompilerParams(vmem_limit_bytes=VMEM_LIMIT),
        name="post_sample",
    )(x, yc, ya, ym, prev_ext, w_o, nf, w_up, fcw, w_down)


def _page_sum_kernel(ck_ref, o_ref):
    x = ck_ref[...]
    o_ref[...] = jnp.concatenate([jnp.sum(x[:, h], axis=-1) for h in range(ATT_HEADS)], axis=-1)


def _page_sum(ck_t, pages_per_step):
    n_pool = ck_t.shape[0]
    assert n_pool % pages_per_step == 0
    return pl.pallas_call(
        _page_sum_kernel,
        grid=(n_pool // pages_per_step,),
        in_specs=[pl.BlockSpec((pages_per_step, ATT_HEADS, HEAD_DIM, PAGE_SIZE), lambda i: (i, 0, 0, 0))],
        out_specs=pl.BlockSpec((pages_per_step, ATT_DIM), lambda i: (i, 0)),
        out_shape=jax.ShapeDtypeStruct((n_pool, ATT_DIM), F32),
        compiler_params=pltpu.CompilerParams(dimension_semantics=("arbitrary",), vmem_limit_bytes=VMEM_LIMIT),
        name="page_sum",
    )(ck_t)


def _sample_gate_kernel(pt_ref, q_ref, ps_ref, sel_ref, km_ref):
    b = pl.program_id(0)
    nfp = km_ref.shape[0]
    n_pages = nfp * PAGES_PER_BLOCK

    def gather(n, carry):
        i0 = pt_ref[b * n_pages + PAGES_PER_BLOCK * n]
        i1 = pt_ref[b * n_pages + PAGES_PER_BLOCK * n + 1]
        km_ref[pl.ds(n, 1), :] = (ps_ref[pl.ds(i0, 1), :] + ps_ref[pl.ds(i1, 1), :]) * (1.0 / MOBA_BLOCK)
        return carry

    lax.fori_loop(0, nfp, gather, 0)
    q = q_ref[...]
    km = km_ref[...]
    t = q.shape[0]
    head_of_lane = lax.broadcasted_iota(jnp.int32, q.shape, 1) // HEAD_DIM
    blk = lax.broadcasted_iota(jnp.int32, (t, nfp), 1).astype(F32)
    lane = lax.broadcasted_iota(jnp.int32, (t, LANES), 1)
    out = jnp.zeros((t, LANES), F32)
    for h in range(ATT_HEADS):
        g = _nt3(jnp.where(head_of_lane == h, q, 0.0), km)
        for kk in range(MOBA_TOPK):
            m = jnp.max(g, axis=-1, keepdims=True)
            first = jnp.min(jnp.where(g == m, blk, float(nfp)), axis=-1, keepdims=True)
            out = jnp.where(lane == h * MOBA_TOPK + kk, first, out)
            g = jnp.where(blk == first, -jnp.inf, g)
    sel_ref[...] = out.astype(jnp.int32)


def _sample_gate(pt_flat, q, page_sum, t, nfp):
    n = q.shape[0]
    return pl.pallas_call(
        _sample_gate_kernel,
        grid_spec=pltpu.PrefetchScalarGridSpec(
            num_scalar_prefetch=1,
            grid=(n // t,),
            in_specs=[pl.BlockSpec((t, ATT_DIM), lambda b, pt: (b, 0)),
                      pl.BlockSpec(page_sum.shape, lambda b, pt: (0, 0), pipeline_mode=pl.Buffered(1))],
            out_specs=pl.BlockSpec((t, LANES), lambda b, pt: (b, 0)),
            scratch_shapes=[pltpu.VMEM((nfp, ATT_DIM), F32)]),
        out_shape=jax.ShapeDtypeStruct((n, LANES), jnp.int32),
        compiler_params=pltpu.CompilerParams(dimension_semantics=("arbitrary",), vmem_limit_bytes=VMEM_LIMIT),
        name="sample_gate",
    )(pt_flat, q, page_sum)


def _sample_attn_kernel(pt_ref, sel_ref, slopes_ref, qt_ref, kt_ref, vt_ref, ck_ref, cv_ref, o_ref,
                        kbuf, vbuf, sem, *, n_pages, past_len):
    b = pl.program_id(0)
    h = pl.program_id(1)
    nb = pl.num_programs(0)
    t_len = qt_ref.shape[-1]
    step = b * ATT_HEADS + h
    slot = step % 2
    per_q = MOBA_TOPK * PAGES_PER_BLOCK
    sel_row = ATT_HEADS * MOBA_TOPK

    def copies(bb, hh, sl, t, kk, p):
        blk = sel_ref[(bb * t_len + t) * sel_row + hh * MOBA_TOPK + kk]
        phys = pt_ref[bb * n_pages + blk * PAGES_PER_BLOCK + p]
        idx = t * per_q + kk * PAGES_PER_BLOCK + p
        return (pltpu.make_async_copy(ck_ref.at[phys, hh], kbuf.at[sl, idx], sem.at[sl, 0]),
                pltpu.make_async_copy(cv_ref.at[phys, hh], vbuf.at[sl, idx], sem.at[sl, 1]))

    def for_all(bb, hh, sl, fn):
        def body(t, carry):
            for kk in range(MOBA_TOPK):
                for p in range(PAGES_PER_BLOCK):
                    for cp in copies(bb, hh, sl, t, kk, p):
                        fn(cp)
            return carry
        lax.fori_loop(0, t_len, body, 0)

    @pl.when(step == 0)
    def _():
        for_all(b, h, slot, lambda cp: cp.start())

    @pl.when(step + 1 < nb * ATT_HEADS)
    def _():
        nxt = step + 1
        for_all(nxt // ATT_HEADS, nxt % ATT_HEADS, 1 - slot, lambda cp: cp.start())

    for_all(b, h, slot, lambda cp: cp.wait())

    slope = slopes_ref[h]
    qt = qt_ref[0, 0]
    kt_new = kt_ref[0, 0]
    vt_new = vt_ref[0, 0]
    key = lax.broadcasted_iota(jnp.int32, (1, PAGE_SIZE), 1)
    new = lax.broadcasted_iota(jnp.int32, (1, t_len), 1)
    out_lane = lax.broadcasted_iota(jnp.int32, (HEAD_DIM, t_len), 1)
    out = jnp.zeros((HEAD_DIM, t_len), F32)
    for t in range(t_len):
        qc = qt[:, t:t + 1]
        scores = []
        for kk in range(MOBA_TOPK):
            blk = sel_ref[(b * t_len + t) * sel_row + h * MOBA_TOPK + kk]
            for p in range(PAGES_PER_BLOCK):
                kt = kbuf[slot, t * per_q + kk * PAGES_PER_BLOCK + p]
                dist = (past_len + t - blk * MOBA_BLOCK - p * PAGE_SIZE - key).astype(F32)
                scores.append(jnp.sum(kt * qc, axis=0, keepdims=True) * SCALE - slope * dist)
        s_new = jnp.sum(kt_new * qc, axis=0, keepdims=True) * SCALE - slope * (t - new).astype(F32)
        s_new = jnp.where(new <= t, s_new, NEG)
        m = jnp.max(s_new, axis=-1, keepdims=True)
        for s in scores:
            m = jnp.maximum(m, jnp.max(s, axis=-1, keepdims=True))
        p_new = jnp.exp(s_new - m)
        denom = jnp.sum(p_new, axis=-1, keepdims=True)
        acc = jnp.zeros((HEAD_DIM, PAGE_SIZE), F32)
        for c, s in enumerate(scores):
            pr = jnp.exp(s - m)
            denom = denom + jnp.sum(pr, axis=-1, keepdims=True)
            acc = acc + vbuf[slot, t * per_q + c] * pr
        o = jnp.sum(acc, axis=-1, keepdims=True) + jnp.sum(vt_new * p_new, axis=-1, keepdims=True)
        out = jnp.where(out_lane == t, o * (1.0 / denom), out)
    o_ref[0, 0] = out


def _sample_attn(pt_flat, sel_flat, slopes, qt, kt, vt, ck_t, cv_t, n_pages):
    nheads, nb, _, t_len = qt.shape
    per_step = t_len * MOBA_TOPK * PAGES_PER_BLOCK
    tok = pl.BlockSpec((1, 1, HEAD_DIM, t_len), lambda b, h, pt, sel: (h, b, 0, 0))
    kernel = functools.partial(_sample_attn_kernel, n_pages=n_pages, past_len=n_pages * PAGE_SIZE)
    return pl.pallas_call(
        kernel,
        grid_spec=pltpu.PrefetchScalarGridSpec(
            num_scalar_prefetch=2,
            grid=(nb, nheads),
            in_specs=[pl.BlockSpec(memory_space=pltpu.SMEM), tok, tok, tok,
                      pl.BlockSpec(memory_space=pl.ANY), pl.BlockSpec(memory_space=pl.ANY)],
            out_specs=tok,
            scratch_shapes=[pltpu.VMEM((2, per_step, HEAD_DIM, PAGE_SIZE), F32),
                            pltpu.VMEM((2, per_step, HEAD_DIM, PAGE_SIZE), F32),
                            pltpu.SemaphoreType.DMA((2, 2))]),
        out_shape=jax.ShapeDtypeStruct((nheads, nb, HEAD_DIM, t_len), F32),
        compiler_params=pltpu.CompilerParams(dimension_semantics=("arbitrary", "arbitrary"),
                                             vmem_limit_bytes=VMEM_LIMIT),
        name="sample_attention",
    )(pt_flat, sel_flat, slopes, qt, kt, vt, ck_t, cv_t)


def _prev_ext(cache, t_len):
    nb, w, c = cache.shape
    return jnp.concatenate([cache, jnp.zeros((nb, t_len - w, c), cache.dtype)], axis=1).reshape(nb * t_len, c)


def kernel(x_prompt, x_sample, mem_prompt, cache_k, cache_v, page_table, cache_conv, cache_ffn_conv, cache_mem_k, cache_mem_v, norm_mix, w_in, conv_w, q_norm, k_norm, memq_norm, mem_norm, w_mem_kv, memk_norm, w_o, norm_ffn, w_up, ffn_conv_w, w_down):
    depth = w_in.shape[0]
    assert depth == 1
    nbp, seq, d_model = x_prompt.shape
    nbs, t_len, _ = x_sample.shape
    n_pages = page_table.shape[1]
    assert t_len == SUBLANES and n_pages % PAGES_PER_BLOCK == 0
    nfp = n_pages // PAGES_PER_BLOCK

    slopes = jnp.exp2(-8.0 * jnp.arange(1, ATT_HEADS + 1, dtype=F32) / ATT_HEADS)
    bd = (jnp.kron(jnp.eye(ATT_HEADS, dtype=F32), jnp.ones((HEAD_DIM, HEAD_DIM), F32)) / HEAD_DIM).astype(BF16)
    nm, nf, mn = norm_mix, norm_ffn, mem_norm
    qg = jnp.tile(q_norm, (1, ATT_HEADS))
    kg = jnp.tile(k_norm, (1, ATT_HEADS))
    mqg = jnp.tile(memq_norm, (1, MEM_HEADS))
    mkg = jnp.tile(memk_norm, (1, MEM_HEADS))
    w_in_b = w_in[0].astype(BF16)
    w_o_b = w_o[0].astype(BF16)
    w_up_b = w_up[0].astype(BF16)
    w_down_b = w_down[0].astype(BF16)
    w_mem_b = w_mem_kv[0].astype(BF16)
    cw, fcw = conv_w[0], ffn_conv_w[0]

    mk_t, mv_t = _memkv(mem_prompt, mn, w_mem_b, mkg, bd[:MEM_DIM, :MEM_DIM])
    mem_len = mk_t.shape[-1]
    yc, k_p, v_p, mq, q_aug, k_aug, v_aug, cs_p = _inproj_prompt(slopes, x_prompt, nm, w_in_b, cw, qg, kg, mqg, bd)
    ya = _moba_prompt(q_aug, k_aug, v_aug)
    ym = _memattn(mq, mk_t, mv_t, 2 * MOBA_BLOCK)
    y_p, fs_p = _post_prompt(x_prompt, yc, ya, ym, w_o_b, nf, w_up_b, fcw, w_down_b)

    n_s = nbs * t_len
    xs = x_sample.reshape(n_s, d_model)
    u_s, yc_s, q_s, k_s, v_s, mq_s = _inproj_sample(xs, _prev_ext(cache_conv[0], t_len), nm, w_in_b, cw,
                                                    qg, kg, mqg, bd)
    ck_t = jnp.transpose(cache_k[0], (0, 2, 3, 1))
    cv_t = jnp.transpose(cache_v[0], (0, 2, 3, 1))
    n_pool = ck_t.shape[0]
    pages_per_step = next((p for p in (2 * SUBLANES, SUBLANES) if n_pool % p == 0), n_pool)
    page_sum = _page_sum(ck_t, pages_per_step)
    pt_flat = page_table.reshape(-1)
    sel = _sample_gate(pt_flat, q_s, page_sum, t_len, nfp)
    sel_flat = sel[:, :ATT_HEADS * MOBA_TOPK].reshape(-1)
    heads_t = lambda a: jnp.transpose(a.reshape(nbs, t_len, ATT_HEADS, HEAD_DIM), (2, 0, 3, 1))
    ya_t = _sample_attn(pt_flat, sel_flat, slopes, heads_t(q_s), heads_t(k_s), heads_t(v_s), ck_t, cv_t, n_pages)
    ya_s = jnp.transpose(ya_t, (1, 3, 0, 2)).reshape(n_s, ATT_DIM).astype(BF16)
    cmk_t = jnp.transpose(cache_mem_k[0], (0, 2, 3, 1)).reshape(nbs, MEM_DIM, mem_len)
    cmv_t = jnp.transpose(cache_mem_v[0], (0, 2, 3, 1)).reshape(nbs, MEM_DIM, mem_len)
    ym_s = _memattn(mq_s.reshape(nbs, t_len, MEM_DIM), cmk_t, cmv_t, t_len).reshape(n_s, MEM_DIM)
    y_s, up_s = _post_sample(xs, yc_s, ya_s, ym_s, _prev_ext(cache_ffn_conv[0], t_len), w_o_b, nf, w_up_b,
                             fcw, w_down_b)

    heads = lambda a, nb, t: a.reshape(1, nb, t, ATT_HEADS, HEAD_DIM)
    tail = lambda a, nb, t: a.reshape(nb, t, -1)[None, :, t - 2:]
    mem_out = lambda a: jnp.transpose(a.reshape(nbp, MEM_HEADS, HEAD_DIM, mem_len), (0, 3, 1, 2))[None]
    return (y_p, y_s.reshape(nbs, t_len, d_model),
            heads(k_p, nbp, seq), heads(v_p, nbp, seq), heads(k_s, nbs, t_len), heads(v_s, nbs, t_len),
            cs_p[None, :, SUBLANES - 2:], tail(u_s, nbs, t_len),
            fs_p[None, :, SUBLANES - 2:], tail(up_s, nbs, t_len),
            mem_out(mk_t), mem_out(mv_t))
```

```python
import functools

import jax
import jax.numpy as jnp
from jax import lax
from jax.experimental import pallas as pl
from jax.experimental.pallas import tpu as pltpu

HEAD_DIM = 64
CONV_DIM = 384
ATT_HEADS = 6
ATT_DIM = ATT_HEADS * HEAD_DIM
MEM_HEADS = 4
MEM_DIM = MEM_HEADS * HEAD_DIM
MOBA_BLOCK = 256
MOBA_TOPK = 3
PAGE_SIZE = 128
PAGES_PER_BLOCK = MOBA_BLOCK // PAGE_SIZE
EPS = 1e-6
SCALE = HEAD_DIM ** -0.5
LOG2E = 1.4426950408889634

LANES = 128
SUBLANES = 8
NEG = -1e30
SEL_LANE0 = HEAD_DIM
ALIBI_LANE0 = SEL_LANE0 + 32
MAX_BLOCKS = ALIBI_LANE0 - SEL_LANE0
FFN_CHUNK = 256
KV_GROUP = 4
VMEM_LIMIT = 48 * 1024 * 1024

F32 = jnp.float32
BF16 = jnp.bfloat16


def _nt(a, b):
    return lax.dot_general(a, b, (((1,), (1,)), ((), ())), preferred_element_type=F32)


def _mm(a, b):
    return jnp.dot(a, b, preferred_element_type=F32)


def _split(x):
    hi = x.astype(BF16)
    lo = (x - hi.astype(F32)).astype(BF16)
    return hi, lo


def _nt3(a, b):
    a_hi, a_lo = _split(a)
    b_hi, b_lo = _split(b)
    return _nt(a_hi, b_hi) + _nt(a_hi, b_lo) + _nt(a_lo, b_hi)


def _rms(x, g):
    return x * lax.rsqrt(jnp.mean(x * x, axis=-1, keepdims=True) + EPS) * g


def _head_rms(x, bd, g):
    hi, lo = _split(x * x)
    ms = _mm(hi, bd) + _mm(lo, bd)
    return x * lax.rsqrt(ms + EPS) * g


def _conv3(u, u1, u2, w):
    return w[0:1] * u2 + w[1:2] * u1 + w[2:3] * u


def _shift_tile(u, prev8):
    row = lax.broadcasted_iota(jnp.int32, u.shape, 0)
    p6 = jnp.broadcast_to(prev8[6:7], u.shape)
    p7 = jnp.broadcast_to(prev8[7:8], u.shape)
    u1 = jnp.where(row == 0, p7, pltpu.roll(u, 1, 0))
    u2 = jnp.where(row == 0, p6, jnp.where(row == 1, p7, pltpu.roll(u, 2, 0)))
    return u1, u2


def _shift_groups(u, prev_ext):
    n = u.shape[0]
    r8 = lax.broadcasted_iota(jnp.int32, u.shape, 0) % SUBLANES
    u1 = jnp.where(r8 == 0, pltpu.roll(prev_ext, n - 1, 0), pltpu.roll(u, 1, 0))
    u2 = jnp.where(r8 < 2, prev_ext, pltpu.roll(u, 2, 0))
    return u1, u2


def _project(x, nm_ref, w_ref, cw_ref, qg_ref, kg_ref, mqg_ref, bd_ref, shift):
    xn, xn_lo = _split(_rms(x, nm_ref[...]))
    if w_ref.dtype == BF16:
        seg = lambda a, b: _mm(xn, w_ref[:, a:b])
    else:
        def seg(a, b):
            w_hi, w_lo = _split(w_ref[:, a:b])
            return _mm(xn, w_hi) + _mm(xn, w_lo) + _mm(xn_lo, w_hi)
    c0, c1, c2 = CONV_DIM, 2 * CONV_DIM, 3 * CONV_DIM
    a1, a2, a3 = c2 + ATT_DIM, c2 + 2 * ATT_DIM, c2 + 3 * ATT_DIM
    u = seg(c0, c1) * seg(c1, c2)
    u1, u2 = shift(u)
    y_conv = seg(0, c0) * _conv3(u, u1, u2, cw_ref[...])
    bd = bd_ref[...]
    q = _head_rms(seg(c2, a1), bd, qg_ref[...])
    k = _head_rms(seg(a1, a2), bd, kg_ref[...])
    v = seg(a2, a3)
    mq = _head_rms(seg(a3, a3 + MEM_DIM), bd[:MEM_DIM, :MEM_DIM], mqg_ref[...])
    return u, y_conv, q, k, v, mq


def _inproj_prompt_kernel(slopes_ref, x_ref, nm_ref, w_ref, cw_ref, qg_ref, kg_ref, mqg_ref, bd_ref,
                          yc_ref, k_ref, v_ref, mq_ref, qa_ref, ka_ref, va_ref, cs_ref,
                          carry_ref, km_ref):
    i = pl.program_id(1)
    tm = x_ref.shape[1]

    @pl.when(i == 0)
    def _():
        carry_ref[...] = jnp.zeros_like(carry_ref)
        km_ref[...] = jnp.zeros_like(km_ref)

    shift = lambda u: _shift_tile(u, carry_ref[...])
    u, y_conv, q, k, v, mq = _project(x_ref[0], nm_ref, w_ref, cw_ref, qg_ref, kg_ref, mqg_ref, bd_ref, shift)
    carry_ref[...] = u[tm - SUBLANES:]
    cs_ref[0] = u[tm - SUBLANES:]
    yc_ref[0] = y_conv.astype(BF16)
    k_ref[0] = k
    v_ref[0] = v
    mq_ref[0] = (mq * SCALE).astype(BF16)

    gate = _nt3(q, km_ref[...])
    lane = lax.broadcasted_iota(jnp.int32, (tm, LANES), 1)
    lane_f = lane.astype(F32)
    row = lax.broadcasted_iota(jnp.int32, (tm, LANES), 0)
    pos = (i * tm + row).astype(F32)
    own = lane == SEL_LANE0 + i
    past = (lane >= SEL_LANE0) & (lane < SEL_LANE0 + i)
    for h in range(ATT_HEADS):
        g = jnp.where(past, gate[:, h * LANES:(h + 1) * LANES], -jnp.inf)
        sel = own
        for _ in range(MOBA_TOPK):
            m = jnp.max(g, axis=-1, keepdims=True)
            first = jnp.min(jnp.where((g == m) & past, lane_f, float(LANES)), axis=-1, keepdims=True)
            pick = (lane_f == first) & (m > -jnp.inf)
            sel = sel | pick
            g = jnp.where(pick, -jnp.inf, g)
        sel_bias = jnp.where(sel, 0.0, NEG)

        pair = slice((h // 2) * LANES, (h // 2 + 1) * LANES)
        head = (lambda t: t[:, pair]) if h % 2 == 0 else (lambda t: pltpu.roll(t[:, pair], HEAD_DIM, 1))
        low = lane < HEAD_DIM
        q_aux = jnp.where(lane < ALIBI_LANE0, sel_bias, jnp.where(lane < ALIBI_LANE0 + 3, 1.0, 0.0))
        qa_ref[0, h] = jnp.where(low, head(q) * (SCALE * LOG2E), q_aux).astype(BF16)

        a = (slopes_ref[h] * LOG2E) * pos
        a_hi = a.astype(BF16).astype(F32)
        a_mid = (a - a_hi).astype(BF16).astype(F32)
        a_lo = a - a_hi - a_mid
        k_aux = jnp.where(own, 1.0, 0.0)
        k_aux = jnp.where(lane == ALIBI_LANE0, a_hi, k_aux)
        k_aux = jnp.where(lane == ALIBI_LANE0 + 1, a_mid, k_aux)
        k_aux = jnp.where(lane == ALIBI_LANE0 + 2, a_lo, k_aux)
        ka_ref[0, h] = jnp.where(low, head(k), k_aux).astype(BF16)
        va_ref[0, h] = jnp.where(low, head(v), jnp.where(lane == HEAD_DIM, 1.0, 0.0)).astype(BF16)

    kmean = jnp.sum(k, axis=0, keepdims=True) * (1.0 / MOBA_BLOCK)
    head_of_lane = lax.broadcasted_iota(jnp.int32, kmean.shape, 1) // HEAD_DIM
    for h in range(ATT_HEADS):
        km_ref[pl.ds(h * LANES + SEL_LANE0 + i, 1), :] = jnp.where(head_of_lane == h, kmean, 0.0)


def _inproj_prompt(slopes, x, nm, w_in, cw, qg, kg, mqg, bd):
    nb, s, d = x.shape
    tm = MOBA_BLOCK
    nt = s // tm
    assert s % tm == 0 and nt <= MAX_BLOCKS
    const = lambda shape: pl.BlockSpec(shape, lambda b, i: (0,) * len(shape), pipeline_mode=pl.Buffered(1))
    tok = lambda w: pl.BlockSpec((1, tm, w), lambda b, i: (b, i, 0))
    aug = pl.BlockSpec((1, ATT_HEADS, tm, LANES), lambda b, i: (b, 0, i, 0))
    aug_shape = jax.ShapeDtypeStruct((nb, ATT_HEADS, s, LANES), BF16)
    return pl.pallas_call(
        _inproj_prompt_kernel,
        grid=(nb, nt),
        in_specs=[pl.BlockSpec(memory_space=pltpu.SMEM), tok(d), const(nm.shape), const(w_in.shape),
                  const(cw.shape), const(qg.shape), const(kg.shape), const(mqg.shape), const(bd.shape)],
        out_specs=[tok(CONV_DIM), tok(ATT_DIM), tok(ATT_DIM), tok(MEM_DIM), aug, aug, aug,
                   pl.BlockSpec((1, SUBLANES, CONV_DIM), lambda b, i: (b, 0, 0))],
        out_shape=[jax.ShapeDtypeStruct((nb, s, CONV_DIM), BF16),
                   jax.ShapeDtypeStruct((nb, s, ATT_DIM), F32),
                   jax.ShapeDtypeStruct((nb, s, ATT_DIM), F32),
                   jax.ShapeDtypeStruct((nb, s, MEM_DIM), BF16),
                   aug_shape, aug_shape, aug_shape,
                   jax.ShapeDtypeStruct((nb, SUBLANES, CONV_DIM), F32)],
        scratch_shapes=[pltpu.VMEM((SUBLANES, CONV_DIM), F32),
                        pltpu.VMEM((ATT_HEADS * LANES, ATT_DIM), F32)],
        compiler_params=pltpu.CompilerParams(dimension_semantics=("arbitrary", "arbitrary"),
                                             vmem_limit_bytes=VMEM_LIMIT),
        name="inproj_prompt",
    )(slopes, x, nm, w_in, cw, qg, kg, mqg, bd)


def _moba_prompt_kernel(qa_ref, ka_ref, va_ref, o_ref, s_ref):
    i = pl.program_id(2)
    tq = qa_ref.shape[2]
    tg = s_ref.shape[-1]
    n_full = (i * tq) // tg
    qs = [qa_ref[0, hh] for hh in range(2)]
    keys = lambda ref, hh, g: ref[0, hh, pl.ds(pl.multiple_of(g * tg, tg), tg), :]

    def scores(g, ms):
        out = []
        for hh in range(2):
            s = _nt(qs[hh], keys(ka_ref, hh, g))
            s_ref[hh, g] = s
            out.append(jnp.maximum(ms[hh], jnp.max(s, axis=-1, keepdims=True)))
        return tuple(out)

    ms = lax.fori_loop(0, n_full, scores, (jnp.full((tq, 1), NEG, F32),) * 2)
    q_pos = i * tq + lax.broadcasted_iota(jnp.int32, (tq, tg), 0)
    k_pos = n_full * tg + lax.broadcasted_iota(jnp.int32, (tq, tg), 1)
    causal = k_pos <= q_pos
    m_rows = []
    for hh in range(2):
        s = jnp.where(causal, _nt(qs[hh], keys(ka_ref, hh, n_full)), NEG)
        s_ref[hh, n_full] = s
        m_rows.append(jnp.maximum(ms[hh], jnp.max(s, axis=-1, keepdims=True)))

    def weigh(g, accs):
        return tuple(accs[hh] + _mm(jnp.exp2(s_ref[hh, g] - m_rows[hh]).astype(BF16), keys(va_ref, hh, g))
                     for hh in range(2))

    accs = lax.fori_loop(0, n_full + 1, weigh, (jnp.zeros((tq, LANES), F32),) * 2)
    outs = [acc * (1.0 / acc[:, HEAD_DIM:HEAD_DIM + 1]) for acc in accs]
    lane = lax.broadcasted_iota(jnp.int32, (tq, LANES), 1)
    o_ref[0] = jnp.where(lane < HEAD_DIM, outs[0], pltpu.roll(outs[1], HEAD_DIM, 1)).astype(BF16)


def _moba_prompt(q_aug, k_aug, v_aug):
    nb, _, s, _ = q_aug.shape
    tq = MOBA_BLOCK
    tg = min(KV_GROUP * MOBA_BLOCK, s)
    assert s % tg == 0
    kv = pl.BlockSpec((1, 2, s, LANES), lambda b, p, i: (b, p, 0, 0))
    return pl.pallas_call(
        _moba_prompt_kernel,
        grid=(nb, ATT_HEADS // 2, s // tq),
        in_specs=[pl.BlockSpec((1, 2, tq, LANES), lambda b, p, i: (b, p, i, 0)), kv, kv],
        out_specs=pl.BlockSpec((1, tq, LANES), lambda b, p, i: (b, i, p)),
        out_shape=jax.ShapeDtypeStruct((nb, s, ATT_DIM), BF16),
        scratch_shapes=[pltpu.VMEM((2, s // tg, tq, tg), F32)],
        compiler_params=pltpu.CompilerParams(dimension_semantics=("arbitrary", "arbitrary", "arbitrary"),
                                             vmem_limit_bytes=VMEM_LIMIT),
        name="moba_prompt",
    )(q_aug, k_aug, v_aug)


def _memkv_kernel(mem_ref, g_ref, w_ref, kg_ref, bd_ref, mk_ref, mv_ref):
    kv = _mm(_rms(mem_ref[0], g_ref[...]).astype(BF16), w_ref[...])
    mk = _head_rms(kv[:, :MEM_DIM], bd_ref[...], kg_ref[...])
    mk_ref[0] = mk.T
    mv_ref[0] = kv[:, MEM_DIM:].T


def _memkv(mem, g, w, kg, bd):
    nb, m, d = mem.shape
    const = lambda a: pl.BlockSpec(a.shape, lambda b: (0,) * a.ndim)
    out = pl.BlockSpec((1, MEM_DIM, m), lambda b: (b, 0, 0))
    return pl.pallas_call(
        _memkv_kernel,
        grid=(nb,),
        in_specs=[pl.BlockSpec((1, m, d), lambda b: (b, 0, 0)), const(g), const(w), const(kg), const(bd)],
        out_specs=[out, out],
        out_shape=[jax.ShapeDtypeStruct((nb, MEM_DIM, m), F32)] * 2,
        compiler_params=pltpu.CompilerParams(dimension_semantics=("arbitrary",)),
        name="memory_kv",
    )(mem, g, w, kg, bd)


def _memattn_kernel(mq_ref, mk_ref, mv_ref, o_ref):
    mq = mq_ref[0]
    mk = mk_ref[0].astype(BF16)
    mv = mv_ref[0].astype(BF16)
    head_of_lane = lax.broadcasted_iota(jnp.int32, mq.shape, 1) // HEAD_DIM
    out = jnp.zeros(mq.shape, F32)
    for h in range(MEM_HEADS):
        mine = head_of_lane == h
        s = _mm(jnp.where(mine, mq, jnp.zeros_like(mq)), mk)
        p = jnp.exp(s - jnp.max(s, axis=-1, keepdims=True))
        inv = 1.0 / jnp.sum(p, axis=-1, keepdims=True)
        out = jnp.where(mine, _nt(p.astype(BF16), mv) * inv, out)
    o_ref[0] = out.astype(BF16)


def _memattn(mq, mk_t, mv_t, tq):
    nb, t, _ = mq.shape
    m = mk_t.shape[-1]
    kv = pl.BlockSpec((1, MEM_DIM, m), lambda b, i: (b, 0, 0))
    return pl.pallas_call(
        _memattn_kernel,
        grid=(nb, t // tq),
        in_specs=[pl.BlockSpec((1, tq, MEM_DIM), lambda b, i: (b, i, 0)), kv, kv],
        out_specs=pl.BlockSpec((1, tq, MEM_DIM), lambda b, i: (b, i, 0)),
        out_shape=jax.ShapeDtypeStruct((nb, t, MEM_DIM), BF16),
        compiler_params=pltpu.CompilerParams(dimension_semantics=("arbitrary", "arbitrary")),
        name="memory_attention",
    )(mq, mk_t, mv_t)


def _post(x, yc, ya, ym, wo_ref, nf_ref, wu_ref, fcw_ref, wd_ref, act_ref, shift, keep):
    h = (x + _mm(yc, wo_ref[0:CONV_DIM]) + _mm(ya, wo_ref[CONV_DIM:CONV_DIM + ATT_DIM])
         + _mm(ym, wo_ref[CONV_DIM + ATT_DIM:]))
    hn = _rms(h, nf_ref[...]).astype(BF16)
    d_ff = wd_ref.shape[0]
    for c in range(d_ff // FFN_CHUNK):
        halves = []
        for base in (0, d_ff):
            cols = slice(base + c * FFN_CHUNK, base + (c + 1) * FFN_CHUNK)
            up = _mm(hn, wu_ref[:, cols])
            u1, u2 = shift(up, cols)
            keep(up, cols)
            halves.append(_conv3(up, u1, u2, fcw_ref[:, cols]))
        a, g = halves
        act = g * (1.0 / (1.0 + jnp.exp(-g))) * a
        act_ref[:, c * FFN_CHUNK:(c + 1) * FFN_CHUNK] = act.astype(BF16)
    return h + _mm(act_ref[...], wd_ref[...])


def _post_prompt_kernel(x_ref, yc_ref, ya_ref, ym_ref, wo_ref, nf_ref, wu_ref, fcw_ref, wd_ref,
                        y_ref, fs_ref, carry_ref, act_ref):
    tm = x_ref.shape[1]

    @pl.when(pl.program_id(1) == 0)
    def _():
        carry_ref[...] = jnp.zeros_like(carry_ref)

    def keep(up, cols):
        carry_ref[:, cols] = up[tm - SUBLANES:]
        fs_ref[0, :, cols] = up[tm - SUBLANES:]

    shift = lambda up, cols: _shift_tile(up, carry_ref[:, cols])
    y_ref[0] = _post(x_ref[0], yc_ref[0], ya_ref[0], ym_ref[0], wo_ref, nf_ref, wu_ref, fcw_ref, wd_ref,
                     act_ref, shift, keep)


def _post_prompt(x, yc, ya, ym, w_o, nf, w_up, fcw, w_down):
    nb, s, d = x.shape
    tm = MOBA_BLOCK
    f2 = w_up.shape[1]
    const = lambda a: pl.BlockSpec(a.shape, lambda b, i: (0,) * a.ndim, pipeline_mode=pl.Buffered(1))
    tok = lambda w: pl.BlockSpec((1, tm, w), lambda b, i: (b, i, 0))
    return pl.pallas_call(
        _post_prompt_kernel,
        grid=(nb, s // tm),
        in_specs=[tok(d), tok(CONV_DIM), tok(ATT_DIM), tok(MEM_DIM),
                  const(w_o), const(nf), const(w_up), const(fcw), const(w_down)],
        out_specs=[tok(d), pl.BlockSpec((1, SUBLANES, f2), lambda b, i: (b, 0, 0))],
        out_shape=[jax.ShapeDtypeStruct((nb, s, d), F32), jax.ShapeDtypeStruct((nb, SUBLANES, f2), F32)],
        scratch_shapes=[pltpu.VMEM((SUBLANES, f2), F32), pltpu.VMEM((tm, f2 // 2), BF16)],
        compiler_params=pltpu.CompilerParams(dimension_semantics=("arbitrary", "arbitrary"),
                                             vmem_limit_bytes=VMEM_LIMIT),
        name="post_prompt",
    )(x, yc, ya, ym, w_o, nf, w_up, fcw, w_down)


def _inproj_sample_kernel(x_ref, pe_ref, nm_ref, w_ref, cw_ref, qg_ref, kg_ref, mqg_ref, bd_ref,
                          u_ref, yc_ref, q_ref, k_ref, v_ref, mq_ref):
    shift = lambda u: _shift_groups(u, pe_ref[...])
    u, y_conv, q, k, v, mq = _project(x_ref[...], nm_ref, w_ref, cw_ref, qg_ref, kg_ref, mqg_ref, bd_ref, shift)
    u_ref[...] = u
    yc_ref[...] = y_conv.astype(BF16)
    q_ref[...] = q
    k_ref[...] = k
    v_ref[...] = v
    mq_ref[...] = (mq * SCALE).astype(BF16)


def _inproj_sample(x, prev_ext, nm, w_in, cw, qg, kg, mqg, bd):
    n = x.shape[0]
    shp = lambda w, dt: jax.ShapeDtypeStruct((n, w), dt)
    return pl.pallas_call(
        _inproj_sample_kernel,
        out_shape=[shp(CONV_DIM, F32), shp(CONV_DIM, BF16), shp(ATT_DIM, F32), shp(ATT_DIM, F32),
                   shp(ATT_DIM, F32), shp(MEM_DIM, BF16)],
        compiler_params=pltpu.CompilerParams(vmem_limit_bytes=VMEM_LIMIT),
        name="inproj_sample",
    )(x, prev_ext, nm, w_in, cw, qg, kg, mqg, bd)


def _post_sample_kernel(x_ref, yc_ref, ya_ref, ym_ref, pe_ref, wo_ref, nf_ref, wu_ref, fcw_ref, wd_ref,
                        y_ref, up_ref, act_ref):
    def keep(up, cols):
        up_ref[:, cols] = up

    shift = lambda up, cols: _shift_groups(up, pe_ref[:, cols])
    y_ref[...] = _post(x_ref[...], yc_ref[...], ya_ref[...], ym_ref[...], wo_ref, nf_ref, wu_ref, fcw_ref,
                       wd_ref, act_ref, shift, keep)


def _post_sample(x, yc, ya, ym, prev_ext, w_o, nf, w_up, fcw, w_down):
    n, d = x.shape
    return pl.pallas_call(
        _post_sample_kernel,
        out_shape=[jax.ShapeDtypeStruct((n, d), F32), jax.ShapeDtypeStruct((n, w_up.shape[1]), F32)],
        scratch_shapes=[pltpu.VMEM((n, w_down.shape[0]), BF16)],
        compiler_params=pltpu.CompilerParams(vmem_limit_bytes=VMEM_LIMIT),
        name="post_sample",
    )(x, yc, ya, ym, prev_ext, w_o, nf, w_up, fcw, w_down)


def _page_sum_kernel(ck_ref, o_ref):
    x = ck_ref[...]
    o_ref[...] = jnp.concatenate([jnp.sum(x[:, h], axis=-1) for h in range(ATT_HEADS)], axis=-1)


def _page_sum(ck_t, pages_per_step):
    n_pool = ck_t.shape[0]
    assert n_pool % pages_per_step == 0
    return pl.pallas_call(
        _page_sum_kernel,
        grid=(n_pool // pages_per_step,),
        in_specs=[pl.BlockSpec((pages_per_step, ATT_HEADS, HEAD_DIM, PAGE_SIZE), lambda i: (i, 0, 0, 0))],
        out_specs=pl.BlockSpec((pages_per_step, ATT_DIM), lambda i: (i, 0)),
        out_shape=jax.ShapeDtypeStruct((n_pool, ATT_DIM), F32),
        compiler_params=pltpu.CompilerParams(dimension_semantics=("arbitrary",), vmem_limit_bytes=VMEM_LIMIT),
        name="page_sum",
    )(ck_t)


def _sample_gate_kernel(pt_ref, q_ref, ps_ref, sel_ref, km_ref):
    b = pl.program_id(0)
    nfp = km_ref.shape[0]
    n_pages = nfp * PAGES_PER_BLOCK

    def gather(n, carry):
        i0 = pt_ref[b * n_pages + PAGES_PER_BLOCK * n]
        i1 = pt_ref[b * n_pages + PAGES_PER_BLOCK * n + 1]
        km_ref[pl.ds(n, 1), :] = (ps_ref[pl.ds(i0, 1), :] + ps_ref[pl.ds(i1, 1), :]) * (1.0 / MOBA_BLOCK)
        return carry

    lax.fori_loop(0, nfp, gather, 0)
    q = q_ref[...]
    km = km_ref[...]
    t = q.shape[0]
    head_of_lane = lax.broadcasted_iota(jnp.int32, q.shape, 1) // HEAD_DIM
    blk = lax.broadcasted_iota(jnp.int32, (t, nfp), 1).astype(F32)
    lane = lax.broadcasted_iota(jnp.int32, (t, LANES), 1)
    out = jnp.zeros((t, LANES), F32)
    for h in range(ATT_HEADS):
        g = _nt3(jnp.where(head_of_lane == h, q, 0.0), km)
        for kk in range(MOBA_TOPK):
            m = jnp.max(g, axis=-1, keepdims=True)
            first = jnp.min(jnp.where(g == m, blk, float(nfp)), axis=-1, keepdims=True)
            out = jnp.where(lane == h * MOBA_TOPK + kk, first, out)
            g = jnp.where(blk == first, -jnp.inf, g)
    sel_ref[...] = out.astype(jnp.int32)


def _sample_gate(pt_flat, q, page_sum, t, nfp):
    n = q.shape[0]
    return pl.pallas_call(
        _sample_gate_kernel,
        grid_spec=pltpu.PrefetchScalarGridSpec(
            num_scalar_prefetch=1,
            grid=(n // t,),
            in_specs=[pl.BlockSpec((t, ATT_DIM), lambda b, pt: (b, 0)),
                      pl.BlockSpec(page_sum.shape, lambda b, pt: (0, 0), pipeline_mode=pl.Buffered(1))],
            out_specs=pl.BlockSpec((t, LANES), lambda b, pt: (b, 0)),
            scratch_shapes=[pltpu.VMEM((nfp, ATT_DIM), F32)]),
        out_shape=jax.ShapeDtypeStruct((n, LANES), jnp.int32),
        compiler_params=pltpu.CompilerParams(dimension_semantics=("arbitrary",), vmem_limit_bytes=VMEM_LIMIT),
        name="sample_gate",
    )(pt_flat, q, page_sum)


def _sample_attn_kernel(pt_ref, sel_ref, slopes_ref, qt_ref, kt_ref, vt_ref, ck_ref, cv_ref, o_ref,
                        kbuf, vbuf, sem, *, n_pages, past_len):
    b = pl.program_id(0)
    h = pl.program_id(1)
    nb = pl.num_programs(0)
    t_len = qt_ref.shape[-1]
    step = b * ATT_HEADS + h
    slot = step % 2
    per_q = MOBA_TOPK * PAGES_PER_BLOCK
    sel_row = ATT_HEADS * MOBA_TOPK

    def copies(bb, hh, sl, t, kk, p):
        blk = sel_ref[(bb * t_len + t) * sel_row + hh * MOBA_TOPK + kk]
        phys = pt_ref[bb * n_pages + blk * PAGES_PER_BLOCK + p]
        idx = t * per_q + kk * PAGES_PER_BLOCK + p
        return (pltpu.make_async_copy(ck_ref.at[phys, hh], kbuf.at[sl, idx], sem.at[sl, 0]),
                pltpu.make_async_copy(cv_ref.at[phys, hh], vbuf.at[sl, idx], sem.at[sl, 1]))

    def for_all(bb, hh, sl, fn):
        def body(t, carry):
            for kk in range(MOBA_TOPK):
                for p in range(PAGES_PER_BLOCK):
                    for cp in copies(bb, hh, sl, t, kk, p):
                        fn(cp)
            return carry
        lax.fori_loop(0, t_len, body, 0)

    @pl.when(step == 0)
    def _():
        for_all(b, h, slot, lambda cp: cp.start())

    @pl.when(step + 1 < nb * ATT_HEADS)
    def _():
        nxt = step + 1
        for_all(nxt // ATT_HEADS, nxt % ATT_HEADS, 1 - slot, lambda cp: cp.start())

    for_all(b, h, slot, lambda cp: cp.wait())

    slope = slopes_ref[h]
    qt = qt_ref[0, 0]
    kt_new = kt_ref[0, 0]
    vt_new = vt_ref[0, 0]
    key = lax.broadcasted_iota(jnp.int32, (1, PAGE_SIZE), 1)
    new = lax.broadcasted_iota(jnp.int32, (1, t_len), 1)
    out_lane = lax.broadcasted_iota(jnp.int32, (HEAD_DIM, t_len), 1)
    out = jnp.zeros((HEAD_DIM, t_len), F32)
    for t in range(t_len):
        qc = qt[:, t:t + 1]
        scores = []
        for kk in range(MOBA_TOPK):
            blk = sel_ref[(b * t_len + t) * sel_row + h * MOBA_TOPK + kk]
            for p in range(PAGES_PER_BLOCK):
                kt = kbuf[slot, t * per_q + kk * PAGES_PER_BLOCK + p]
                dist = (past_len + t - blk * MOBA_BLOCK - p * PAGE_SIZE - key).astype(F32)
                scores.append(jnp.sum(kt * qc, axis=0, keepdims=True) * SCALE - slope * dist)
        s_new = jnp.sum(kt_new * qc, axis=0, keepdims=True) * SCALE - slope * (t - new).astype(F32)
        s_new = jnp.where(new <= t, s_new, NEG)
        m = jnp.max(s_new, axis=-1, keepdims=True)
        for s in scores:
            m = jnp.maximum(m, jnp.max(s, axis=-1, keepdims=True))
        p_new = jnp.exp(s_new - m)
        denom = jnp.sum(p_new, axis=-1, keepdims=True)
        acc = jnp.zeros((HEAD_DIM, PAGE_SIZE), F32)
        for c, s in enumerate(scores):
            pr = jnp.exp(s - m)
            denom = denom + jnp.sum(pr, axis=-1, keepdims=True)
            acc = acc + vbuf[slot, t * per_q + c] * pr
        o = jnp.sum(acc, axis=-1, keepdims=True) + jnp.sum(vt_new * p_new, axis=-1, keepdims=True)
        out = jnp.where(out_lane == t, o * (1.0 / denom), out)
    o_ref[0, 0] = out


def _sample_attn(pt_flat, sel_flat, slopes, qt, kt, vt, ck_t, cv_t, n_pages):
    nheads, nb, _, t_len = qt.shape
    per_step = t_len * MOBA_TOPK * PAGES_PER_BLOCK
    tok = pl.BlockSpec((1, 1, HEAD_DIM, t_len), lambda b, h, pt, sel: (h, b, 0, 0))
    kernel = functools.partial(_sample_attn_kernel, n_pages=n_pages, past_len=n_pages * PAGE_SIZE)
    return pl.pallas_call(
        kernel,
        grid_spec=pltpu.PrefetchScalarGridSpec(
            num_scalar_prefetch=2,
            grid=(nb, nheads),
            in_specs=[pl.BlockSpec(memory_space=pltpu.SMEM), tok, tok, tok,
                      pl.BlockSpec(memory_space=pl.ANY), pl.BlockSpec(memory_space=pl.ANY)],
            out_specs=tok,
            scratch_shapes=[pltpu.VMEM((2, per_step, HEAD_DIM, PAGE_SIZE), F32),
                            pltpu.VMEM((2, per_step, HEAD_DIM, PAGE_SIZE), F32),
                            pltpu.SemaphoreType.DMA((2, 2))]),
        out_shape=jax.ShapeDtypeStruct((nheads, nb, HEAD_DIM, t_len), F32),
        compiler_params=pltpu.CompilerParams(dimension_semantics=("arbitrary", "arbitrary"),
                                             vmem_limit_bytes=VMEM_LIMIT),
        name="sample_attention",
    )(pt_flat, sel_flat, slopes, qt, kt, vt, ck_t, cv_t)


def _prev_ext(cache, t_len):
    nb, w, c = cache.shape
    return jnp.concatenate([cache, jnp.zeros((nb, t_len - w, c), cache.dtype)], axis=1).reshape(nb * t_len, c)


def kernel(x_prompt, x_sample, mem_prompt, cache_k, cache_v, page_table, cache_conv, cache_ffn_conv, cache_mem_k, cache_mem_v, norm_mix, w_in, conv_w, q_norm, k_norm, memq_norm, mem_norm, w_mem_kv, memk_norm, w_o, norm_ffn, w_up, ffn_conv_w, w_down):
    depth = w_in.shape[0]
    assert depth == 1
    nbp, seq, d_model = x_prompt.shape
    nbs, t_len, _ = x_sample.shape
    n_pages = page_table.shape[1]
    assert t_len == SUBLANES and n_pages % PAGES_PER_BLOCK == 0
    nfp = n_pages // PAGES_PER_BLOCK

    slopes = jnp.exp2(-8.0 * jnp.arange(1, ATT_HEADS + 1, dtype=F32) / ATT_HEADS)
    bd = (jnp.kron(jnp.eye(ATT_HEADS, dtype=F32), jnp.ones((HEAD_DIM, HEAD_DIM), F32)) / HEAD_DIM).astype(BF16)
    nm, nf, mn = norm_mix, norm_ffn, mem_norm
    qg = jnp.tile(q_norm, (1, ATT_HEADS))
    kg = jnp.tile(k_norm, (1, ATT_HEADS))
    mqg = jnp.tile(memq_norm, (1, MEM_HEADS))
    mkg = jnp.tile(memk_norm, (1, MEM_HEADS))
    w_in_b = w_in[0].astype(BF16)
    w_o_b = w_o[0].astype(BF16)
    w_up_b = w_up[0].astype(BF16)
    w_down_b = w_down[0].astype(BF16)
    w_mem_b = w_mem_kv[0].astype(BF16)
    cw, fcw = conv_w[0], ffn_conv_w[0]

    mk_t, mv_t = _memkv(mem_prompt, mn, w_mem_b, mkg, bd[:MEM_DIM, :MEM_DIM])
    mem_len = mk_t.shape[-1]
    yc, k_p, v_p, mq, q_aug, k_aug, v_aug, cs_p = _inproj_prompt(slopes, x_prompt, nm, w_in_b, cw, qg, kg, mqg, bd)
    ya = _moba_prompt(q_aug, k_aug, v_aug)
    ym = _memattn(mq, mk_t, mv_t, 2 * MOBA_BLOCK)
    y_p, fs_p = _post_prompt(x_prompt, yc, ya, ym, w_o_b, nf, w_up_b, fcw, w_down_b)

    n_s = nbs * t_len
    xs = x_sample.reshape(n_s, d_model)
    u_s, yc_s, q_s, k_s, v_s, mq_s = _inproj_sample(xs, _prev_ext(cache_conv[0], t_len), nm, w_in[0], cw,
                                                    qg, kg, mqg, bd)
    ck_t = jnp.transpose(cache_k[0], (0, 2, 3, 1))
    cv_t = jnp.transpose(cache_v[0], (0, 2, 3, 1))
    n_pool = ck_t.shape[0]
    pages_per_step = next((p for p in (2 * SUBLANES, SUBLANES) if n_pool % p == 0), n_pool)
    page_sum = _page_sum(ck_t, pages_per_step)
    pt_flat = page_table.reshape(-1)
    sel = _sample_gate(pt_flat, q_s, page_sum, t_len, nfp)
    sel_flat = sel[:, :ATT_HEADS * MOBA_TOPK].reshape(-1)
    heads_t = lambda a: jnp.transpose(a.reshape(nbs, t_len, ATT_HEADS, HEAD_DIM), (2, 0, 3, 1))
    ya_t = _sample_attn(pt_flat, sel_flat, slopes, heads_t(q_s), heads_t(k_s), heads_t(v_s), ck_t, cv_t, n_pages)
    ya_s = jnp.transpose(ya_t, (1, 3, 0, 2)).reshape(n_s, ATT_DIM).astype(BF16)
    cmk_t = jnp.transpose(cache_mem_k[0], (0, 2, 3, 1)).reshape(nbs, MEM_DIM, mem_len)
    cmv_t = jnp.transpose(cache_mem_v[0], (0, 2, 3, 1)).reshape(nbs, MEM_DIM, mem_len)
    ym_s = _memattn(mq_s.reshape(nbs, t_len, MEM_DIM), cmk_t, cmv_t, t_len).reshape(n_s, MEM_DIM)
    y_s, up_s = _post_sample(xs, yc_s, ya_s, ym_s, _prev_ext(cache_ffn_conv[0], t_len), w_o_b, nf, w_up_b,
                             fcw, w_down_b)

    heads = lambda a, nb, t: a.reshape(1, nb, t, ATT_HEADS, HEAD_DIM)
    tail = lambda a, nb, t: a.reshape(nb, t, -1)[None, :, t - 2:]
    mem_out = lambda a: jnp.transpose(a.reshape(nbp, MEM_HEADS, HEAD_DIM, mem_len), (0, 3, 1, 2))[None]
    return (y_p, y_s.reshape(nbs, t_len, d_model),
            heads(k_p, nbp, seq), heads(v_p, nbp, seq), heads(k_s, nbs, t_len), heads(v_s, nbs, t_len),
            cs_p[None, :, SUBLANES - 2:], tail(u_s, nbs, t_len),
            fs_p[None, :, SUBLANES - 2:], tail(up_s, nbs, t_len),
            mem_out(mk_t), mem_out(mv_t))
```

```python
import functools

import jax
import jax.numpy as jnp
from jax import lax
from jax.experimental import pallas as pl
from jax.experimental.pallas import tpu as pltpu

HEAD_DIM = 64
CONV_DIM = 384
ATT_HEADS = 6
ATT_DIM = ATT_HEADS * HEAD_DIM
MEM_HEADS = 4
MEM_DIM = MEM_HEADS * HEAD_DIM
MOBA_BLOCK = 256
MOBA_TOPK = 3
PAGE_SIZE = 128
PAGES_PER_BLOCK = MOBA_BLOCK // PAGE_SIZE
EPS = 1e-6
SCALE = HEAD_DIM ** -0.5
LOG2E = 1.4426950408889634

LANES = 128
SUBLANES = 8
NEG = -1e30
SEL_LANE0 = HEAD_DIM
ALIBI_LANE0 = SEL_LANE0 + 32
MAX_BLOCKS = ALIBI_LANE0 - SEL_LANE0
FFN_CHUNK = 256
KV_GROUP = 4
POST_TILE = 512
PAGE_SUM_PAGES = (80, 64, 40, 32, 16, 8)
VMEM_LIMIT = 48 * 1024 * 1024

F32 = jnp.float32
BF16 = jnp.bfloat16


def _nt(a, b):
    return lax.dot_general(a, b, (((1,), (1,)), ((), ())), preferred_element_type=F32)


def _mm(a, b):
    return jnp.dot(a, b, preferred_element_type=F32)


def _split(x):
    hi = x.astype(BF16)
    lo = (x - hi.astype(F32)).astype(BF16)
    return hi, lo


def _nt3(a, b):
    a_hi, a_lo = _split(a)
    b_hi, b_lo = _split(b)
    return _nt(a_hi, b_hi) + _nt(a_hi, b_lo) + _nt(a_lo, b_hi)


def _rms(x, g):
    return x * lax.rsqrt(jnp.mean(x * x, axis=-1, keepdims=True) + EPS) * g


def _head_rms(x, bd, g, split=True):
    hi, lo = _split(x * x)
    ms = _mm(hi, bd) + _mm(lo, bd) if split else _mm(hi, bd)
    return x * lax.rsqrt(ms + EPS) * g


def _conv3(u, u1, u2, w):
    return w[0:1] * u2 + w[1:2] * u1 + w[2:3] * u


def _shift_tile(u, prev8):
    row = lax.broadcasted_iota(jnp.int32, u.shape, 0)
    p6 = jnp.broadcast_to(prev8[6:7], u.shape)
    p7 = jnp.broadcast_to(prev8[7:8], u.shape)
    u1 = jnp.where(row == 0, p7, pltpu.roll(u, 1, 0))
    u2 = jnp.where(row == 0, p6, jnp.where(row == 1, p7, pltpu.roll(u, 2, 0)))
    return u1, u2


def _shift_groups(u, prev_ext):
    n = u.shape[0]
    r8 = lax.broadcasted_iota(jnp.int32, u.shape, 0) % SUBLANES
    u1 = jnp.where(r8 == 0, pltpu.roll(prev_ext, n - 1, 0), pltpu.roll(u, 1, 0))
    u2 = jnp.where(r8 < 2, prev_ext, pltpu.roll(u, 2, 0))
    return u1, u2


def _project(x, nm_ref, w_ref, cw_ref, qg_ref, kg_ref, mqg_ref, bd_ref, shift):
    xn, xn_lo = _split(_rms(x, nm_ref[...]))
    if w_ref.dtype == BF16:
        seg = lambda a, b: _mm(xn, w_ref[:, a:b])
    else:
        def seg(a, b):
            w_hi, w_lo = _split(w_ref[:, a:b])
            return _mm(xn, w_hi) + _mm(xn, w_lo) + _mm(xn_lo, w_hi)
    c0, c1, c2 = CONV_DIM, 2 * CONV_DIM, 3 * CONV_DIM
    a1, a2, a3 = c2 + ATT_DIM, c2 + 2 * ATT_DIM, c2 + 3 * ATT_DIM
    u = seg(c0, c1) * seg(c1, c2)
    u1, u2 = shift(u)
    y_conv = seg(0, c0) * _conv3(u, u1, u2, cw_ref[...])
    bd = bd_ref[...]
    split = w_ref.dtype != BF16
    q = _head_rms(seg(c2, a1), bd, qg_ref[...], split)
    k = _head_rms(seg(a1, a2), bd, kg_ref[...], split)
    v = seg(a2, a3)
    mq = _head_rms(seg(a3, a3 + MEM_DIM), bd[:MEM_DIM, :MEM_DIM], mqg_ref[...], split)
    return u, y_conv, q, k, v, mq


def _inproj_prompt_kernel(slopes_ref, x_ref, nm_ref, w_ref, cw_ref, qg_ref, kg_ref, mqg_ref, bd_ref,
                          yc_ref, k_ref, v_ref, mq_ref, qa_ref, ka_ref, va_ref, cs_ref,
                          carry_ref, km_ref):
    i = pl.program_id(1)
    tm = x_ref.shape[1]

    @pl.when(i == 0)
    def _():
        carry_ref[...] = jnp.zeros_like(carry_ref)
        km_ref[...] = jnp.zeros_like(km_ref)

    shift = lambda u: _shift_tile(u, carry_ref[...])
    u, y_conv, q, k, v, mq = _project(x_ref[0], nm_ref, w_ref, cw_ref, qg_ref, kg_ref, mqg_ref, bd_ref, shift)
    carry_ref[...] = u[tm - SUBLANES:]
    cs_ref[0] = u[tm - SUBLANES:]
    yc_ref[0] = y_conv.astype(BF16)
    k_ref[0] = k
    v_ref[0] = v
    mq_ref[0] = (mq * SCALE).astype(BF16)

    gate = _nt3(q, km_ref[...])
    lane = lax.broadcasted_iota(jnp.int32, (tm, LANES), 1)
    lane_f = lane.astype(F32)
    row = lax.broadcasted_iota(jnp.int32, (tm, LANES), 0)
    pos = (i * tm + row).astype(F32)
    own = lane == SEL_LANE0 + i
    past = (lane >= SEL_LANE0) & (lane < SEL_LANE0 + i)
    for h in range(ATT_HEADS):
        g = jnp.where(past, gate[:, h * LANES:(h + 1) * LANES], -jnp.inf)
        sel = jnp.zeros(own.shape, jnp.bool_)
        for _ in range(MOBA_TOPK):
            m = jnp.max(g, axis=-1, keepdims=True)
            first = jnp.min(jnp.where((g == m) & past, lane_f, float(LANES)), axis=-1, keepdims=True)
            pick = (lane_f == first) & (m > -jnp.inf)
            sel = sel | pick
            g = jnp.where(pick, -jnp.inf, g)
        sel_bias = jnp.where(sel, 0.0, NEG)

        pair = slice((h // 2) * LANES, (h // 2 + 1) * LANES)
        head = (lambda t: t[:, pair]) if h % 2 == 0 else (lambda t: pltpu.roll(t[:, pair], HEAD_DIM, 1))
        low = lane < HEAD_DIM
        q_aux = jnp.where(lane < ALIBI_LANE0, sel_bias, jnp.where(lane < ALIBI_LANE0 + 3, 1.0, 0.0))
        qa_ref[0, h] = jnp.where(low, head(q) * (SCALE * LOG2E), q_aux).astype(BF16)

        a = (slopes_ref[h] * LOG2E) * pos
        a_hi = a.astype(BF16).astype(F32)
        a_mid = (a - a_hi).astype(BF16).astype(F32)
        a_lo = a - a_hi - a_mid
        k_aux = jnp.where(own, 1.0, 0.0)
        k_aux = jnp.where(lane == ALIBI_LANE0, a_hi, k_aux)
        k_aux = jnp.where(lane == ALIBI_LANE0 + 1, a_mid, k_aux)
        k_aux = jnp.where(lane == ALIBI_LANE0 + 2, a_lo, k_aux)
        ka_ref[0, h] = jnp.where(low, head(k), k_aux).astype(BF16)
        va_ref[0, h] = jnp.where(low, head(v), jnp.where(lane == HEAD_DIM, 1.0, 0.0)).astype(BF16)

    kmean = jnp.sum(k, axis=0, keepdims=True) * (1.0 / MOBA_BLOCK)
    head_of_lane = lax.broadcasted_iota(jnp.int32, kmean.shape, 1) // HEAD_DIM
    for h in range(ATT_HEADS):
        km_ref[pl.ds(h * LANES + SEL_LANE0 + i, 1), :] = jnp.where(head_of_lane == h, kmean, 0.0)


def _inproj_prompt(slopes, x, nm, w_in, cw, qg, kg, mqg, bd):
    nb, s, d = x.shape
    tm = MOBA_BLOCK
    nt = s // tm
    assert s % tm == 0 and nt <= MAX_BLOCKS
    const = lambda shape: pl.BlockSpec(shape, lambda b, i: (0,) * len(shape), pipeline_mode=pl.Buffered(1))
    tok = lambda w: pl.BlockSpec((1, tm, w), lambda b, i: (b, i, 0))
    aug = pl.BlockSpec((1, ATT_HEADS, tm, LANES), lambda b, i: (b, 0, i, 0))
    aug_shape = jax.ShapeDtypeStruct((nb, ATT_HEADS, s, LANES), BF16)
    return pl.pallas_call(
        _inproj_prompt_kernel,
        grid=(nb, nt),
        in_specs=[pl.BlockSpec(memory_space=pltpu.SMEM), tok(d), const(nm.shape), const(w_in.shape),
                  const(cw.shape), const(qg.shape), const(kg.shape), const(mqg.shape), const(bd.shape)],
        out_specs=[tok(CONV_DIM), tok(ATT_DIM), tok(ATT_DIM), tok(MEM_DIM), aug, aug, aug,
                   pl.BlockSpec((1, SUBLANES, CONV_DIM), lambda b, i: (b, 0, 0))],
        out_shape=[jax.ShapeDtypeStruct((nb, s, CONV_DIM), BF16),
                   jax.ShapeDtypeStruct((nb, s, ATT_DIM), F32),
                   jax.ShapeDtypeStruct((nb, s, ATT_DIM), F32),
                   jax.ShapeDtypeStruct((nb, s, MEM_DIM), BF16),
                   aug_shape, aug_shape, aug_shape,
                   jax.ShapeDtypeStruct((nb, SUBLANES, CONV_DIM), F32)],
        scratch_shapes=[pltpu.VMEM((SUBLANES, CONV_DIM), F32),
                        pltpu.VMEM((ATT_HEADS * LANES, ATT_DIM), F32)],
        compiler_params=pltpu.CompilerParams(dimension_semantics=("arbitrary", "arbitrary"),
                                             vmem_limit_bytes=VMEM_LIMIT),
        name="inproj_prompt",
    )(slopes, x, nm, w_in, cw, qg, kg, mqg, bd)


def _moba_prompt_kernel(qa_ref, ka_ref, va_ref, o_ref, s_ref, mg_ref, m_ref, acc_ref):
    i = pl.program_id(2)
    tq = qa_ref.shape[2]
    tg = s_ref.shape[-1]
    n_groups = (i * tq + tg - 1) // tg
    qs = [qa_ref[0, hh] for hh in range(2)]
    keys = lambda ref, hh, g: ref[0, hh, pl.ds(pl.multiple_of(g * tg, tg), tg), :]
    lane = lax.broadcasted_iota(jnp.int32, (tq, LANES), 1)

    def score(g, slot):
        for hh in range(2):
            s = _nt(qs[hh], keys(ka_ref, hh, g))
            s_ref[slot, hh] = s
            mg_ref[slot, hh] = jnp.max(s, axis=-1, keepdims=True)

    def weigh(g, slot):
        for hh in range(2):
            m_old = m_ref[hh]
            m_new = jnp.maximum(m_old, mg_ref[slot, hh])
            p = jnp.exp2(s_ref[slot, hh] - m_new).astype(BF16)
            acc_ref[hh] = jnp.exp2(m_old - m_new) * acc_ref[hh] + _mm(p, keys(va_ref, hh, g))
            m_ref[hh] = m_new

    score(0, 0)
    own = pl.ds(pl.multiple_of(i * tq, tq), tq)
    causal = lax.broadcasted_iota(jnp.int32, (tq, tq), 0) >= lax.broadcasted_iota(jnp.int32, (tq, tq), 1)
    sel_lanes = (lane >= SEL_LANE0) & (lane < ALIBI_LANE0)
    for hh in range(2):
        k_own = ka_ref[0, hh, own, :]
        k_own = jnp.where(sel_lanes, jnp.zeros_like(k_own), k_own)
        s = jnp.where(causal, _nt(qs[hh], k_own), NEG)
        m = jnp.max(s, axis=-1, keepdims=True)
        m_ref[hh] = m
        acc_ref[hh] = _mm(jnp.exp2(s - m).astype(BF16), va_ref[0, hh, own, :])

    def trip(g, carry):
        weigh(g, g % 2)
        score(g + 1, (g + 1) % 2)
        return carry

    lax.fori_loop(0, n_groups - 1, trip, 0)
    last = jnp.maximum(n_groups - 1, 0)
    weigh(last, last % 2)
    outs = [acc_ref[hh] * (1.0 / acc_ref[hh][:, HEAD_DIM:HEAD_DIM + 1]) for hh in range(2)]
    o_ref[0] = jnp.where(lane < HEAD_DIM, outs[0], pltpu.roll(outs[1], HEAD_DIM, 1)).astype(BF16)


def _moba_prompt(q_aug, k_aug, v_aug):
    nb, _, s, _ = q_aug.shape
    tq = MOBA_BLOCK
    tg = min(KV_GROUP * MOBA_BLOCK, s)
    assert s % tg == 0
    kv = pl.BlockSpec((1, 2, s, LANES), lambda b, p, i: (b, p, 0, 0))
    return pl.pallas_call(
        _moba_prompt_kernel,
        grid=(nb, ATT_HEADS // 2, s // tq),
        in_specs=[pl.BlockSpec((1, 2, tq, LANES), lambda b, p, i: (b, p, i, 0)), kv, kv],
        out_specs=pl.BlockSpec((1, tq, LANES), lambda b, p, i: (b, i, p)),
        out_shape=jax.ShapeDtypeStruct((nb, s, ATT_DIM), BF16),
        scratch_shapes=[pltpu.VMEM((2, 2, tq, tg), F32), pltpu.VMEM((2, 2, tq, 1), F32),
                        pltpu.VMEM((2, tq, 1), F32), pltpu.VMEM((2, tq, LANES), F32)],
        compiler_params=pltpu.CompilerParams(dimension_semantics=("arbitrary", "arbitrary", "arbitrary"),
                                             vmem_limit_bytes=VMEM_LIMIT),
        name="moba_prompt",
    )(q_aug, k_aug, v_aug)


def _memkv_kernel(mem_ref, g_ref, w_ref, kg_ref, bd_ref, mk_ref, mv_ref):
    kv = _mm(_rms(mem_ref[0], g_ref[...]).astype(BF16), w_ref[...])
    mk = _head_rms(kv[:, :MEM_DIM], bd_ref[...], kg_ref[...])
    mk_ref[0] = mk.T
    mv_ref[0] = kv[:, MEM_DIM:].T


def _memkv(mem, g, w, kg, bd):
    nb, m, d = mem.shape
    const = lambda a: pl.BlockSpec(a.shape, lambda b: (0,) * a.ndim)
    out = pl.BlockSpec((1, MEM_DIM, m), lambda b: (b, 0, 0))
    return pl.pallas_call(
        _memkv_kernel,
        grid=(nb,),
        in_specs=[pl.BlockSpec((1, m, d), lambda b: (b, 0, 0)), const(g), const(w), const(kg), const(bd)],
        out_specs=[out, out],
        out_shape=[jax.ShapeDtypeStruct((nb, MEM_DIM, m), F32)] * 2,
        compiler_params=pltpu.CompilerParams(dimension_semantics=("arbitrary",)),
        name="memory_kv",
    )(mem, g, w, kg, bd)


def _memattn_kernel(mq_ref, mk_ref, mv_ref, o_ref):
    mq = mq_ref[0]
    mk = mk_ref[0].astype(BF16)
    mv = mv_ref[0].astype(BF16)
    head_of_lane = lax.broadcasted_iota(jnp.int32, mq.shape, 1) // HEAD_DIM
    out = jnp.zeros(mq.shape, F32)
    for h in range(MEM_HEADS):
        mine = head_of_lane == h
        s = _mm(jnp.where(mine, mq, jnp.zeros_like(mq)), mk)
        p = jnp.exp(s - jnp.max(s, axis=-1, keepdims=True))
        inv = 1.0 / jnp.sum(p, axis=-1, keepdims=True)
        out = jnp.where(mine, _nt(p.astype(BF16), mv) * inv, out)
    o_ref[0] = out.astype(BF16)


def _memattn(mq, mk_t, mv_t, tq):
    nb, t, _ = mq.shape
    m = mk_t.shape[-1]
    kv = pl.BlockSpec((1, MEM_DIM, m), lambda b, i: (b, 0, 0))
    return pl.pallas_call(
        _memattn_kernel,
        grid=(nb, t // tq),
        in_specs=[pl.BlockSpec((1, tq, MEM_DIM), lambda b, i: (b, i, 0)), kv, kv],
        out_specs=pl.BlockSpec((1, tq, MEM_DIM), lambda b, i: (b, i, 0)),
        out_shape=jax.ShapeDtypeStruct((nb, t, MEM_DIM), BF16),
        compiler_params=pltpu.CompilerParams(dimension_semantics=("arbitrary", "arbitrary")),
        name="memory_attention",
    )(mq, mk_t, mv_t)


def _post(x, yc, ya, ym, wo_ref, nf_ref, wu_ref, fcw_ref, wd_ref, act_ref, shift, keep):
    h = (x + _mm(yc, wo_ref[0:CONV_DIM]) + _mm(ya, wo_ref[CONV_DIM:CONV_DIM + ATT_DIM])
         + _mm(ym, wo_ref[CONV_DIM + ATT_DIM:]))
    hn = _rms(h, nf_ref[...]).astype(BF16)
    d_ff = wd_ref.shape[0]
    for c in range(d_ff // FFN_CHUNK):
        halves = []
        for base in (0, d_ff):
            cols = slice(base + c * FFN_CHUNK, base + (c + 1) * FFN_CHUNK)
            up = _mm(hn, wu_ref[:, cols])
            u1, u2 = shift(up, cols)
            keep(up, cols)
            halves.append(_conv3(up, u1, u2, fcw_ref[:, cols]))
        a, g = halves
        act = g * (1.0 / (1.0 + jnp.exp(-g))) * a
        act_ref[:, c * FFN_CHUNK:(c + 1) * FFN_CHUNK] = act.astype(BF16)
    return h + _mm(act_ref[...], wd_ref[...])


def _post_prompt_kernel(x_ref, yc_ref, ya_ref, ym_ref, wo_ref, nf_ref, wu_ref, fcw_ref, wd_ref,
                        y_ref, fs_ref, carry_ref, act_ref):
    tm = x_ref.shape[1]

    @pl.when(pl.program_id(1) == 0)
    def _():
        carry_ref[...] = jnp.zeros_like(carry_ref)

    def keep(up, cols):
        carry_ref[:, cols] = up[tm - SUBLANES:]
        fs_ref[0, :, cols] = up[tm - SUBLANES:]

    shift = lambda up, cols: _shift_tile(up, carry_ref[:, cols])
    y_ref[0] = _post(x_ref[0], yc_ref[0], ya_ref[0], ym_ref[0], wo_ref, nf_ref, wu_ref, fcw_ref, wd_ref,
                     act_ref, shift, keep)


def _post_prompt(x, yc, ya, ym, w_o, nf, w_up, fcw, w_down):
    nb, s, d = x.shape
    tm = POST_TILE if s % POST_TILE == 0 else MOBA_BLOCK
    f2 = w_up.shape[1]
    const = lambda a: pl.BlockSpec(a.shape, lambda b, i: (0,) * a.ndim, pipeline_mode=pl.Buffered(1))
    tok = lambda w: pl.BlockSpec((1, tm, w), lambda b, i: (b, i, 0))
    return pl.pallas_call(
        _post_prompt_kernel,
        grid=(nb, s // tm),
        in_specs=[tok(d), tok(CONV_DIM), tok(ATT_DIM), tok(MEM_DIM),
                  const(w_o), const(nf), const(w_up), const(fcw), const(w_down)],
        out_specs=[tok(d), pl.BlockSpec((1, SUBLANES, f2), lambda b, i: (b, 0, 0))],
        out_shape=[jax.ShapeDtypeStruct((nb, s, d), F32), jax.ShapeDtypeStruct((nb, SUBLANES, f2), F32)],
        scratch_shapes=[pltpu.VMEM((SUBLANES, f2), F32), pltpu.VMEM((tm, f2 // 2), BF16)],
        compiler_params=pltpu.CompilerParams(dimension_semantics=("arbitrary", "arbitrary"),
                                             vmem_limit_bytes=VMEM_LIMIT),
        name="post_prompt",
    )(x, yc, ya, ym, w_o, nf, w_up, fcw, w_down)


def _inproj_sample_kernel(x_ref, pe_ref, nm_ref, w_ref, cw_ref, qg_ref, kg_ref, mqg_ref, bd_ref,
                          u_ref, yc_ref, q_ref, k_ref, v_ref, mq_ref):
    shift = lambda u: _shift_groups(u, pe_ref[...])
    u, y_conv, q, k, v, mq = _project(x_ref[...], nm_ref, w_ref, cw_ref, qg_ref, kg_ref, mqg_ref, bd_ref, shift)
    u_ref[...] = u
    yc_ref[...] = y_conv.astype(BF16)
    q_ref[...] = q
    k_ref[...] = k
    v_ref[...] = v
    mq_ref[...] = (mq * SCALE).astype(BF16)


def _inproj_sample(x, prev_ext, nm, w_in, cw, qg, kg, mqg, bd):
    n = x.shape[0]
    shp = lambda w, dt: jax.ShapeDtypeStruct((n, w), dt)
    return pl.pallas_call(
        _inproj_sample_kernel,
        out_shape=[shp(CONV_DIM, F32), shp(CONV_DIM, BF16), shp(ATT_DIM, F32), shp(ATT_DIM, F32),
                   shp(ATT_DIM, F32), shp(MEM_DIM, BF16)],
        compiler_params=pltpu.CompilerParams(vmem_limit_bytes=VMEM_LIMIT),
        name="inproj_sample",
    )(x, prev_ext, nm, w_in, cw, qg, kg, mqg, bd)


def _post_sample_kernel(x_ref, yc_ref, ya_ref, ym_ref, pe_ref, wo_ref, nf_ref, wu_ref, fcw_ref, wd_ref,
                        y_ref, up_ref, act_ref):
    def keep(up, cols):
        up_ref[:, cols] = up

    shift = lambda up, cols: _shift_groups(up, pe_ref[:, cols])
    y_ref[...] = _post(x_ref[...], yc_ref[...], ya_ref[...], ym_ref[...], wo_ref, nf_ref, wu_ref, fcw_ref,
                       wd_ref, act_ref, shift, keep)


def _post_sample(x, yc, ya, ym, prev_ext, w_o, nf, w_up, fcw, w_down):
    n, d = x.shape
    return pl.pallas_call(
        _post_sample_kernel,
        out_shape=[jax.ShapeDtypeStruct((n, d), F32), jax.ShapeDtypeStruct((n, w_up.shape[1]), F32)],
        scratch_shapes=[pltpu.VMEM((n, w_down.shape[0]), BF16)],
        compiler_params=pltpu.CompilerParams(vmem_limit_bytes=VMEM_LIMIT),
        name="post_sample",
    )(x, yc, ya, ym, prev_ext, w_o, nf, w_up, fcw, w_down)


def _page_sum_kernel(ck_ref, o_ref):
    x = ck_ref[...]
    o_ref[...] = jnp.concatenate([jnp.sum(x[:, h], axis=-1) for h in range(ATT_HEADS)], axis=-1)


def _page_sum(ck_t, pages_per_step):
    n_pool = ck_t.shape[0]
    assert n_pool % pages_per_step == 0
    return pl.pallas_call(
        _page_sum_kernel,
        grid=(n_pool // pages_per_step,),
        in_specs=[pl.BlockSpec((pages_per_step, ATT_HEADS, HEAD_DIM, PAGE_SIZE), lambda i: (i, 0, 0, 0))],
        out_specs=pl.BlockSpec((pages_per_step, ATT_DIM), lambda i: (i, 0)),
        out_shape=jax.ShapeDtypeStruct((n_pool, ATT_DIM), F32),
        compiler_params=pltpu.CompilerParams(dimension_semantics=("arbitrary",), vmem_limit_bytes=VMEM_LIMIT),
        name="page_sum",
    )(ck_t)


def _sample_gate_kernel(pt_ref, q_ref, ps_ref, sel_ref, km_ref):
    b = pl.program_id(0)
    nfp = km_ref.shape[0]
    n_pages = nfp * PAGES_PER_BLOCK

    def gather(n, carry):
        i0 = pt_ref[b * n_pages + PAGES_PER_BLOCK * n]
        i1 = pt_ref[b * n_pages + PAGES_PER_BLOCK * n + 1]
        km_ref[pl.ds(n, 1), :] = (ps_ref[pl.ds(i0, 1), :] + ps_ref[pl.ds(i1, 1), :]) * (1.0 / MOBA_BLOCK)
        return carry

    lax.fori_loop(0, nfp, gather, 0)
    q = q_ref[...]
    km = km_ref[...]
    t = q.shape[0]
    head_of_lane = lax.broadcasted_iota(jnp.int32, q.shape, 1) // HEAD_DIM
    blk = lax.broadcasted_iota(jnp.int32, (t, nfp), 1).astype(F32)
    lane = lax.broadcasted_iota(jnp.int32, (t, LANES), 1)
    out = jnp.zeros((t, LANES), F32)
    for h in range(ATT_HEADS):
        g = _nt3(jnp.where(head_of_lane == h, q, 0.0), km)
        for kk in range(MOBA_TOPK):
            m = jnp.max(g, axis=-1, keepdims=True)
            first = jnp.min(jnp.where(g == m, blk, float(nfp)), axis=-1, keepdims=True)
            out = jnp.where(lane == h * MOBA_TOPK + kk, first, out)
            g = jnp.where(blk == first, -jnp.inf, g)
    sel_ref[...] = out.astype(jnp.int32)


def _sample_gate(pt_flat, q, page_sum, t, nfp):
    n = q.shape[0]
    return pl.pallas_call(
        _sample_gate_kernel,
        grid_spec=pltpu.PrefetchScalarGridSpec(
            num_scalar_prefetch=1,
            grid=(n // t,),
            in_specs=[pl.BlockSpec((t, ATT_DIM), lambda b, pt: (b, 0)),
                      pl.BlockSpec(page_sum.shape, lambda b, pt: (0, 0), pipeline_mode=pl.Buffered(1))],
            out_specs=pl.BlockSpec((t, LANES), lambda b, pt: (b, 0)),
            scratch_shapes=[pltpu.VMEM((nfp, ATT_DIM), F32)]),
        out_shape=jax.ShapeDtypeStruct((n, LANES), jnp.int32),
        compiler_params=pltpu.CompilerParams(dimension_semantics=("arbitrary",), vmem_limit_bytes=VMEM_LIMIT),
        name="sample_gate",
    )(pt_flat, q, page_sum)


def _sample_attn_kernel(pt_ref, sel_ref, slopes_ref, qt_ref, kt_ref, vt_ref, ck_ref, cv_ref, o_ref,
                        kbuf, vbuf, sem, *, n_pages, past_len):
    b = pl.program_id(0)
    h = pl.program_id(1)
    nb = pl.num_programs(0)
    t_len = qt_ref.shape[-1]
    step = b * ATT_HEADS + h
    slot = step % 2
    per_q = MOBA_TOPK * PAGES_PER_BLOCK
    sel_row = ATT_HEADS * MOBA_TOPK

    def copies(bb, hh, sl, t, kk, p):
        blk = sel_ref[(bb * t_len + t) * sel_row + hh * MOBA_TOPK + kk]
        phys = pt_ref[bb * n_pages + blk * PAGES_PER_BLOCK + p]
        idx = t * per_q + kk * PAGES_PER_BLOCK + p
        return (pltpu.make_async_copy(ck_ref.at[phys, hh], kbuf.at[sl, idx], sem.at[sl, 0]),
                pltpu.make_async_copy(cv_ref.at[phys, hh], vbuf.at[sl, idx], sem.at[sl, 1]))

    def for_all(bb, hh, sl, fn):
        def body(t, carry):
            for kk in range(MOBA_TOPK):
                for p in range(PAGES_PER_BLOCK):
                    for cp in copies(bb, hh, sl, t, kk, p):
                        fn(cp)
            return carry
        lax.fori_loop(0, t_len, body, 0)

    @pl.when(step == 0)
    def _():
        for_all(b, h, slot, lambda cp: cp.start())

    @pl.when(step + 1 < nb * ATT_HEADS)
    def _():
        nxt = step + 1
        for_all(nxt // ATT_HEADS, nxt % ATT_HEADS, 1 - slot, lambda cp: cp.start())

    for_all(b, h, slot, lambda cp: cp.wait())

    slope = slopes_ref[h]
    qt = qt_ref[0, 0]
    kt_new = kt_ref[0, 0]
    vt_new = vt_ref[0, 0]
    key = lax.broadcasted_iota(jnp.int32, (1, PAGE_SIZE), 1)
    new = lax.broadcasted_iota(jnp.int32, (1, t_len), 1)
    out_lane = lax.broadcasted_iota(jnp.int32, (HEAD_DIM, t_len), 1)
    out = jnp.zeros((HEAD_DIM, t_len), F32)
    for t in range(t_len):
        qc = qt[:, t:t + 1]
        scores = []
        for kk in range(MOBA_TOPK):
            blk = sel_ref[(b * t_len + t) * sel_row + h * MOBA_TOPK + kk]
            for p in range(PAGES_PER_BLOCK):
                kt = kbuf[slot, t * per_q + kk * PAGES_PER_BLOCK + p]
                dist = (past_len + t - blk * MOBA_BLOCK - p * PAGE_SIZE - key).astype(F32)
                scores.append(jnp.sum(kt * qc, axis=0, keepdims=True) * SCALE - slope * dist)
        s_new = jnp.sum(kt_new * qc, axis=0, keepdims=True) * SCALE - slope * (t - new).astype(F32)
        s_new = jnp.where(new <= t, s_new, NEG)
        m = jnp.max(s_new, axis=-1, keepdims=True)
        for s in scores:
            m = jnp.maximum(m, jnp.max(s, axis=-1, keepdims=True))
        p_new = jnp.exp(s_new - m)
        denom = jnp.sum(p_new, axis=-1, keepdims=True)
        acc = jnp.zeros((HEAD_DIM, PAGE_SIZE), F32)
        for c, s in enumerate(scores):
            pr = jnp.exp(s - m)
            denom = denom + jnp.sum(pr, axis=-1, keepdims=True)
            acc = acc + vbuf[slot, t * per_q + c] * pr
        o = jnp.sum(acc, axis=-1, keepdims=True) + jnp.sum(vt_new * p_new, axis=-1, keepdims=True)
        out = jnp.where(out_lane == t, o * (1.0 / denom), out)
    o_ref[0, 0] = out


def _sample_attn(pt_flat, sel_flat, slopes, qt, kt, vt, ck_t, cv_t, n_pages):
    nheads, nb, _, t_len = qt.shape
    per_step = t_len * MOBA_TOPK * PAGES_PER_BLOCK
    tok = pl.BlockSpec((1, 1, HEAD_DIM, t_len), lambda b, h, pt, sel: (h, b, 0, 0))
    kernel = functools.partial(_sample_attn_kernel, n_pages=n_pages, past_len=n_pages * PAGE_SIZE)
    return pl.pallas_call(
        kernel,
        grid_spec=pltpu.PrefetchScalarGridSpec(
            num_scalar_prefetch=2,
            grid=(nb, nheads),
            in_specs=[pl.BlockSpec(memory_space=pltpu.SMEM), tok, tok, tok,
                      pl.BlockSpec(memory_space=pl.ANY), pl.BlockSpec(memory_space=pl.ANY)],
            out_specs=tok,
            scratch_shapes=[pltpu.VMEM((2, per_step, HEAD_DIM, PAGE_SIZE), F32),
                            pltpu.VMEM((2, per_step, HEAD_DIM, PAGE_SIZE), F32),
                            pltpu.SemaphoreType.DMA((2, 2))]),
        out_shape=jax.ShapeDtypeStruct((nheads, nb, HEAD_DIM, t_len), F32),
        compiler_params=pltpu.CompilerParams(dimension_semantics=("arbitrary", "arbitrary"),
                                             vmem_limit_bytes=VMEM_LIMIT),
        name="sample_attention",
    )(pt_flat, sel_flat, slopes, qt, kt, vt, ck_t, cv_t)


def _prev_ext(cache, t_len):
    nb, w, c = cache.shape
    return jnp.concatenate([cache, jnp.zeros((nb, t_len - w, c), cache.dtype)], axis=1).reshape(nb * t_len, c)


def kernel(x_prompt, x_sample, mem_prompt, cache_k, cache_v, page_table, cache_conv, cache_ffn_conv, cache_mem_k, cache_mem_v, norm_mix, w_in, conv_w, q_norm, k_norm, memq_norm, mem_norm, w_mem_kv, memk_norm, w_o, norm_ffn, w_up, ffn_conv_w, w_down):
    depth = w_in.shape[0]
    assert depth == 1
    nbp, seq, d_model = x_prompt.shape
    nbs, t_len, _ = x_sample.shape
    n_pages = page_table.shape[1]
    assert t_len == SUBLANES and n_pages % PAGES_PER_BLOCK == 0
    nfp = n_pages // PAGES_PER_BLOCK

    slopes = jnp.exp2(-8.0 * jnp.arange(1, ATT_HEADS + 1, dtype=F32) / ATT_HEADS)
    bd = (jnp.kron(jnp.eye(ATT_HEADS, dtype=F32), jnp.ones((HEAD_DIM, HEAD_DIM), F32)) / HEAD_DIM).astype(BF16)
    nm, nf, mn = norm_mix, norm_ffn, mem_norm
    qg = jnp.tile(q_norm, (1, ATT_HEADS))
    kg = jnp.tile(k_norm, (1, ATT_HEADS))
    mqg = jnp.tile(memq_norm, (1, MEM_HEADS))
    mkg = jnp.tile(memk_norm, (1, MEM_HEADS))
    w_in_b = w_in[0].astype(BF16)
    w_o_b = w_o[0].astype(BF16)
    w_up_b = w_up[0].astype(BF16)
    w_down_b = w_down[0].astype(BF16)
    w_mem_b = w_mem_kv[0].astype(BF16)
    cw, fcw = conv_w[0], ffn_conv_w[0]

    mk_t, mv_t = _memkv(mem_prompt, mn, w_mem_b, mkg, bd[:MEM_DIM, :MEM_DIM])
    mem_len = mk_t.shape[-1]
    yc, k_p, v_p, mq, q_aug, k_aug, v_aug, cs_p = _inproj_prompt(slopes, x_prompt, nm, w_in_b, cw, qg, kg, mqg, bd)
    ya = _moba_prompt(q_aug, k_aug, v_aug)
    ym = _memattn(mq, mk_t, mv_t, 2 * MOBA_BLOCK)
    y_p, fs_p = _post_prompt(x_prompt, yc, ya, ym, w_o_b, nf, w_up_b, fcw, w_down_b)

    n_s = nbs * t_len
    xs = x_sample.reshape(n_s, d_model)
    u_s, yc_s, q_s, k_s, v_s, mq_s = _inproj_sample(xs, _prev_ext(cache_conv[0], t_len), nm, w_in[0], cw,
                                                    qg, kg, mqg, bd)
    ck_t = jnp.transpose(cache_k[0], (0, 2, 3, 1))
    cv_t = jnp.transpose(cache_v[0], (0, 2, 3, 1))
    n_pool = ck_t.shape[0]
    pages_per_step = next((p for p in PAGE_SUM_PAGES if n_pool % p == 0), n_pool)
    page_sum = _page_sum(ck_t, pages_per_step)
    pt_flat = page_table.reshape(-1)
    sel = _sample_gate(pt_flat, q_s, page_sum, t_len, nfp)
    sel_flat = sel[:, :ATT_HEADS * MOBA_TOPK].reshape(-1)
    heads_t = lambda a: jnp.transpose(a.reshape(nbs, t_len, ATT_HEADS, HEAD_DIM), (2, 0, 3, 1))
    ya_t = _sample_attn(pt_flat, sel_flat, slopes, heads_t(q_s), heads_t(k_s), heads_t(v_s), ck_t, cv_t, n_pages)
    ya_s = jnp.transpose(ya_t, (1, 3, 0, 2)).reshape(n_s, ATT_DIM).astype(BF16)
    cmk_t = jnp.transpose(cache_mem_k[0], (0, 2, 3, 1)).reshape(nbs, MEM_DIM, mem_len)
    cmv_t = jnp.transpose(cache_mem_v[0], (0, 2, 3, 1)).reshape(nbs, MEM_DIM, mem_len)
    ym_s = _memattn(mq_s.reshape(nbs, t_len, MEM_DIM), cmk_t, cmv_t, t_len).reshape(n_s, MEM_DIM)
    y_s, up_s = _post_sample(xs, yc_s, ya_s, ym_s, _prev_ext(cache_ffn_conv[0], t_len), w_o_b, nf, w_up_b,
                             fcw, w_down_b)

    heads = lambda a, nb, t: a.reshape(1, nb, t, ATT_HEADS, HEAD_DIM)
    tail = lambda a, nb, t: a.reshape(nb, t, -1)[None, :, t - 2:]
    mem_out = lambda a: jnp.transpose(a.reshape(nbp, MEM_HEADS, HEAD_DIM, mem_len), (0, 3, 1, 2))[None]
    return (y_p, y_s.reshape(nbs, t_len, d_model),
            heads(k_p, nbp, seq), heads(v_p, nbp, seq), heads(k_s, nbs, t_len), heads(v_s, nbs, t_len),
            cs_p[None, :, SUBLANES - 2:], tail(u_s, nbs, t_len),
            fs_p[None, :, SUBLANES - 2:], tail(up_s, nbs, t_len),
            mem_out(mk_t), mem_out(mv_t))
```

```python
import functools

import jax
import jax.numpy as jnp
from jax import lax
from jax.experimental import pallas as pl
from jax.experimental.pallas import tpu as pltpu

HEAD_DIM = 64
CONV_DIM = 384
ATT_HEADS = 6
ATT_DIM = ATT_HEADS * HEAD_DIM
MEM_HEADS = 4
MEM_DIM = MEM_HEADS * HEAD_DIM
MOBA_BLOCK = 256
MOBA_TOPK = 3
PAGE_SIZE = 128
PAGES_PER_BLOCK = MOBA_BLOCK // PAGE_SIZE
EPS = 1e-6
SCALE = HEAD_DIM ** -0.5
LOG2E = 1.4426950408889634

LANES = 128
SUBLANES = 8
NEG = -1e30
SEL_LANE0 = HEAD_DIM
ALIBI_LANE0 = SEL_LANE0 + 32
MAX_BLOCKS = ALIBI_LANE0 - SEL_LANE0
FFN_CHUNK = 256
KV_GROUP = 8
POST_TILE = 512
PAGE_SUM_PAGES = (80, 64, 40, 32, 16, 8)
VMEM_LIMIT = 48 * 1024 * 1024
VMEM_LIMIT_POST = 56 * 1024 * 1024

F32 = jnp.float32
BF16 = jnp.bfloat16


def _nt(a, b):
    return lax.dot_general(a, b, (((1,), (1,)), ((), ())), preferred_element_type=F32)


def _mm(a, b):
    return jnp.dot(a, b, preferred_element_type=F32)


def _split(x):
    hi = x.astype(BF16)
    lo = (x - hi.astype(F32)).astype(BF16)
    return hi, lo


def _nt3(a, b):
    a_hi, a_lo = _split(a)
    b_hi, b_lo = _split(b)
    return _nt(a_hi, b_hi) + _nt(a_hi, b_lo) + _nt(a_lo, b_hi)


def _rms(x, g):
    return x * lax.rsqrt(jnp.mean(x * x, axis=-1, keepdims=True) + EPS) * g


def _head_rms(x, bd, g, split=True):
    hi, lo = _split(x * x)
    ms = _mm(hi, bd) + _mm(lo, bd) if split else _mm(hi, bd)
    return x * lax.rsqrt(ms + EPS) * g


def _conv3(u, u1, u2, w):
    return w[0:1] * u2 + w[1:2] * u1 + w[2:3] * u


def _shift_tile(u, prev8):
    row = lax.broadcasted_iota(jnp.int32, u.shape, 0)
    p6 = jnp.broadcast_to(prev8[6:7], u.shape)
    p7 = jnp.broadcast_to(prev8[7:8], u.shape)
    u1 = jnp.where(row == 0, p7, pltpu.roll(u, 1, 0))
    u2 = jnp.where(row == 0, p6, jnp.where(row == 1, p7, pltpu.roll(u, 2, 0)))
    return u1, u2


def _shift_groups(u, prev_ext):
    n = u.shape[0]
    r8 = lax.broadcasted_iota(jnp.int32, u.shape, 0) % SUBLANES
    u1 = jnp.where(r8 == 0, pltpu.roll(prev_ext, n - 1, 0), pltpu.roll(u, 1, 0))
    u2 = jnp.where(r8 < 2, prev_ext, pltpu.roll(u, 2, 0))
    return u1, u2


def _project(x, nm_ref, w_ref, cw_ref, qg_ref, kg_ref, mqg_ref, bd_ref, shift):
    xn, xn_lo = _split(_rms(x, nm_ref[...]))
    if w_ref.dtype == BF16:
        seg = lambda a, b: _mm(xn, w_ref[:, a:b])
    else:
        def seg(a, b):
            w_hi, w_lo = _split(w_ref[:, a:b])
            return _mm(xn, w_hi) + _mm(xn, w_lo) + _mm(xn_lo, w_hi)
    c0, c1, c2 = CONV_DIM, 2 * CONV_DIM, 3 * CONV_DIM
    a1, a2, a3 = c2 + ATT_DIM, c2 + 2 * ATT_DIM, c2 + 3 * ATT_DIM
    u = seg(c0, c1) * seg(c1, c2)
    u1, u2 = shift(u)
    y_conv = seg(0, c0) * _conv3(u, u1, u2, cw_ref[...])
    bd = bd_ref[...]
    split = w_ref.dtype != BF16
    q = _head_rms(seg(c2, a1), bd, qg_ref[...], split)
    k = _head_rms(seg(a1, a2), bd, kg_ref[...], split)
    v = seg(a2, a3)
    mq = _head_rms(seg(a3, a3 + MEM_DIM), bd[:MEM_DIM, :MEM_DIM], mqg_ref[...], split)
    return u, y_conv, q, k, v, mq


def _inproj_prompt_kernel(slopes_ref, x_ref, nm_ref, w_ref, cw_ref, qg_ref, kg_ref, mqg_ref, bd_ref,
                          yc_ref, k_ref, v_ref, mq_ref, qa_ref, ka_ref, va_ref, cs_ref,
                          carry_ref, km_ref):
    i = pl.program_id(1)
    tm = x_ref.shape[1]

    @pl.when(i == 0)
    def _():
        carry_ref[...] = jnp.zeros_like(carry_ref)
        km_ref[...] = jnp.zeros_like(km_ref)

    shift = lambda u: _shift_tile(u, carry_ref[...])
    u, y_conv, q, k, v, mq = _project(x_ref[0], nm_ref, w_ref, cw_ref, qg_ref, kg_ref, mqg_ref, bd_ref, shift)
    carry_ref[...] = u[tm - SUBLANES:]
    cs_ref[0] = u[tm - SUBLANES:]
    yc_ref[0] = y_conv.astype(BF16)
    k_ref[0] = k
    v_ref[0] = v
    mq_ref[0] = (mq * SCALE).astype(BF16)

    gate = _nt3(q, km_ref[...])
    lane = lax.broadcasted_iota(jnp.int32, (tm, LANES), 1)
    lane_f = lane.astype(F32)
    row = lax.broadcasted_iota(jnp.int32, (tm, LANES), 0)
    pos = (i * tm + row).astype(F32)
    own = lane == SEL_LANE0 + i
    past = (lane >= SEL_LANE0) & (lane < SEL_LANE0 + i)
    for h in range(ATT_HEADS):
        g = jnp.where(past, gate[:, h * LANES:(h + 1) * LANES], -jnp.inf)
        sel = jnp.zeros(own.shape, jnp.bool_)
        for _ in range(MOBA_TOPK):
            m = jnp.max(g, axis=-1, keepdims=True)
            first = jnp.min(jnp.where((g == m) & past, lane_f, float(LANES)), axis=-1, keepdims=True)
            pick = (lane_f == first) & (m > -jnp.inf)
            sel = sel | pick
            g = jnp.where(pick, -jnp.inf, g)
        sel_bias = jnp.where(sel, 0.0, NEG)

        pair = slice((h // 2) * LANES, (h // 2 + 1) * LANES)
        head = (lambda t: t[:, pair]) if h % 2 == 0 else (lambda t: pltpu.roll(t[:, pair], HEAD_DIM, 1))
        low = lane < HEAD_DIM
        q_aux = jnp.where(lane < ALIBI_LANE0, sel_bias, jnp.where(lane < ALIBI_LANE0 + 3, 1.0, 0.0))
        qa_ref[0, h] = jnp.where(low, head(q) * (SCALE * LOG2E), q_aux).astype(BF16)

        a = (slopes_ref[h] * LOG2E) * pos
        a_hi = a.astype(BF16).astype(F32)
        a_mid = (a - a_hi).astype(BF16).astype(F32)
        a_lo = a - a_hi - a_mid
        k_aux = jnp.where(own, 1.0, 0.0)
        k_aux = jnp.where(lane == ALIBI_LANE0, a_hi, k_aux)
        k_aux = jnp.where(lane == ALIBI_LANE0 + 1, a_mid, k_aux)
        k_aux = jnp.where(lane == ALIBI_LANE0 + 2, a_lo, k_aux)
        ka_ref[0, h] = jnp.where(low, head(k), k_aux).astype(BF16)
        va_ref[0, h] = jnp.where(low, head(v), jnp.where(lane == HEAD_DIM, 1.0, 0.0)).astype(BF16)

    kmean = jnp.sum(k, axis=0, keepdims=True) * (1.0 / MOBA_BLOCK)
    head_of_lane = lax.broadcasted_iota(jnp.int32, kmean.shape, 1) // HEAD_DIM
    for h in range(ATT_HEADS):
        km_ref[pl.ds(h * LANES + SEL_LANE0 + i, 1), :] = jnp.where(head_of_lane == h, kmean, 0.0)


def _inproj_prompt(slopes, x, nm, w_in, cw, qg, kg, mqg, bd):
    nb, s, d = x.shape
    tm = MOBA_BLOCK
    nt = s // tm
    assert s % tm == 0 and nt <= MAX_BLOCKS
    const = lambda shape: pl.BlockSpec(shape, lambda b, i: (0,) * len(shape), pipeline_mode=pl.Buffered(1))
    tok = lambda w: pl.BlockSpec((1, tm, w), lambda b, i: (b, i, 0))
    aug = pl.BlockSpec((1, ATT_HEADS, tm, LANES), lambda b, i: (b, 0, i, 0))
    aug_shape = jax.ShapeDtypeStruct((nb, ATT_HEADS, s, LANES), BF16)
    return pl.pallas_call(
        _inproj_prompt_kernel,
        grid=(nb, nt),
        in_specs=[pl.BlockSpec(memory_space=pltpu.SMEM), tok(d), const(nm.shape), const(w_in.shape),
                  const(cw.shape), const(qg.shape), const(kg.shape), const(mqg.shape), const(bd.shape)],
        out_specs=[tok(CONV_DIM), tok(ATT_DIM), tok(ATT_DIM), tok(MEM_DIM), aug, aug, aug,
                   pl.BlockSpec((1, SUBLANES, CONV_DIM), lambda b, i: (b, 0, 0))],
        out_shape=[jax.ShapeDtypeStruct((nb, s, CONV_DIM), BF16),
                   jax.ShapeDtypeStruct((nb, s, ATT_DIM), F32),
                   jax.ShapeDtypeStruct((nb, s, ATT_DIM), F32),
                   jax.ShapeDtypeStruct((nb, s, MEM_DIM), BF16),
                   aug_shape, aug_shape, aug_shape,
                   jax.ShapeDtypeStruct((nb, SUBLANES, CONV_DIM), F32)],
        scratch_shapes=[pltpu.VMEM((SUBLANES, CONV_DIM), F32),
                        pltpu.VMEM((ATT_HEADS * LANES, ATT_DIM), F32)],
        compiler_params=pltpu.CompilerParams(dimension_semantics=("arbitrary", "arbitrary"),
                                             vmem_limit_bytes=VMEM_LIMIT),
        name="inproj_prompt",
    )(slopes, x, nm, w_in, cw, qg, kg, mqg, bd)


def _moba_prompt_kernel(qa_ref, ka_ref, va_ref, o_ref, s_ref, mg_ref, m_ref, acc_ref):
    i = pl.program_id(2)
    tq = qa_ref.shape[2]
    tg = s_ref.shape[-1]
    n_groups = (i * tq + tg - 1) // tg
    qs = [qa_ref[0, hh] for hh in range(2)]
    keys = lambda ref, hh, g: ref[0, hh, pl.ds(pl.multiple_of(g * tg, tg), tg), :]
    lane = lax.broadcasted_iota(jnp.int32, (tq, LANES), 1)

    def score(g, slot):
        for hh in range(2):
            s = _nt(qs[hh], keys(ka_ref, hh, g))
            s_ref[slot, hh] = s
            mg_ref[slot, hh] = jnp.max(s, axis=-1, keepdims=True)

    def weigh(g, slot):
        for hh in range(2):
            m_old = m_ref[hh]
            m_new = jnp.maximum(m_old, mg_ref[slot, hh])
            p = jnp.exp2(s_ref[slot, hh] - m_new).astype(BF16)
            acc_ref[hh] = jnp.exp2(m_old - m_new) * acc_ref[hh] + _mm(p, keys(va_ref, hh, g))
            m_ref[hh] = m_new

    score(0, 0)
    own = pl.ds(pl.multiple_of(i * tq, tq), tq)
    causal = lax.broadcasted_iota(jnp.int32, (tq, tq), 0) >= lax.broadcasted_iota(jnp.int32, (tq, tq), 1)
    sel_lanes = (lane >= SEL_LANE0) & (lane < ALIBI_LANE0)
    for hh in range(2):
        k_own = ka_ref[0, hh, own, :]
        k_own = jnp.where(sel_lanes, jnp.zeros_like(k_own), k_own)
        s = jnp.where(causal, _nt(qs[hh], k_own), NEG)
        m = jnp.max(s, axis=-1, keepdims=True)
        m_ref[hh] = m
        acc_ref[hh] = _mm(jnp.exp2(s - m).astype(BF16), va_ref[0, hh, own, :])

    def trip(g, carry):
        weigh(g, g % 2)
        score(g + 1, (g + 1) % 2)
        return carry

    lax.fori_loop(0, n_groups - 1, trip, 0)
    last = jnp.maximum(n_groups - 1, 0)
    weigh(last, last % 2)
    outs = [acc_ref[hh] * (1.0 / acc_ref[hh][:, HEAD_DIM:HEAD_DIM + 1]) for hh in range(2)]
    o_ref[0] = jnp.where(lane < HEAD_DIM, outs[0], pltpu.roll(outs[1], HEAD_DIM, 1)).astype(BF16)


def _moba_prompt(q_aug, k_aug, v_aug):
    nb, _, s, _ = q_aug.shape
    tq = MOBA_BLOCK
    tg = min(KV_GROUP * MOBA_BLOCK, s)
    assert s % tg == 0
    kv = pl.BlockSpec((1, 2, s, LANES), lambda b, p, i: (b, p, 0, 0))
    return pl.pallas_call(
        _moba_prompt_kernel,
        grid=(nb, ATT_HEADS // 2, s // tq),
        in_specs=[pl.BlockSpec((1, 2, tq, LANES), lambda b, p, i: (b, p, i, 0)), kv, kv],
        out_specs=pl.BlockSpec((1, tq, LANES), lambda b, p, i: (b, i, p)),
        out_shape=jax.ShapeDtypeStruct((nb, s, ATT_DIM), BF16),
        scratch_shapes=[pltpu.VMEM((2, 2, tq, tg), F32), pltpu.VMEM((2, 2, tq, 1), F32),
                        pltpu.VMEM((2, tq, 1), F32), pltpu.VMEM((2, tq, LANES), F32)],
        compiler_params=pltpu.CompilerParams(dimension_semantics=("arbitrary", "arbitrary", "arbitrary"),
                                             vmem_limit_bytes=VMEM_LIMIT),
        name="moba_prompt",
    )(q_aug, k_aug, v_aug)


def _memkv_kernel(mem_ref, g_ref, w_ref, kg_ref, bd_ref, mk_ref, mv_ref):
    kv = _mm(_rms(mem_ref[0], g_ref[...]).astype(BF16), w_ref[...])
    mk = _head_rms(kv[:, :MEM_DIM], bd_ref[...], kg_ref[...])
    mk_ref[0] = mk.T
    mv_ref[0] = kv[:, MEM_DIM:].T


def _memkv(mem, g, w, kg, bd):
    nb, m, d = mem.shape
    const = lambda a: pl.BlockSpec(a.shape, lambda b: (0,) * a.ndim)
    out = pl.BlockSpec((1, MEM_DIM, m), lambda b: (b, 0, 0))
    return pl.pallas_call(
        _memkv_kernel,
        grid=(nb,),
        in_specs=[pl.BlockSpec((1, m, d), lambda b: (b, 0, 0)), const(g), const(w), const(kg), const(bd)],
        out_specs=[out, out],
        out_shape=[jax.ShapeDtypeStruct((nb, MEM_DIM, m), F32)] * 2,
        compiler_params=pltpu.CompilerParams(dimension_semantics=("arbitrary",)),
        name="memory_kv",
    )(mem, g, w, kg, bd)


def _memattn_kernel(mq_ref, mk_ref, mv_ref, o_ref):
    mq = mq_ref[0]
    mk = mk_ref[0].astype(BF16)
    mv = mv_ref[0].astype(BF16)
    head_of_lane = lax.broadcasted_iota(jnp.int32, mq.shape, 1) // HEAD_DIM
    out = jnp.zeros(mq.shape, F32)
    for h in range(MEM_HEADS):
        mine = head_of_lane == h
        s = _mm(jnp.where(mine, mq, jnp.zeros_like(mq)), mk)
        p = jnp.exp(s - jnp.max(s, axis=-1, keepdims=True))
        inv = 1.0 / jnp.sum(p, axis=-1, keepdims=True)
        out = jnp.where(mine, _nt(p.astype(BF16), mv) * inv, out)
    o_ref[0] = out.astype(BF16)


def _memattn(mq, mk_t, mv_t, tq):
    nb, t, _ = mq.shape
    m = mk_t.shape[-1]
    kv = pl.BlockSpec((1, MEM_DIM, m), lambda b, i: (b, 0, 0))
    return pl.pallas_call(
        _memattn_kernel,
        grid=(nb, t // tq),
        in_specs=[pl.BlockSpec((1, tq, MEM_DIM), lambda b, i: (b, i, 0)), kv, kv],
        out_specs=pl.BlockSpec((1, tq, MEM_DIM), lambda b, i: (b, i, 0)),
        out_shape=jax.ShapeDtypeStruct((nb, t, MEM_DIM), BF16),
        compiler_params=pltpu.CompilerParams(dimension_semantics=("arbitrary", "arbitrary")),
        name="memory_attention",
    )(mq, mk_t, mv_t)


def _post(x, yc, ya, ym, wo_ref, nf_ref, wu_ref, fcw_ref, wd_ref, act_ref, shift, keep, side_work=None):
    h = (x + _mm(yc, wo_ref[0:CONV_DIM]) + _mm(ya, wo_ref[CONV_DIM:CONV_DIM + ATT_DIM])
         + _mm(ym, wo_ref[CONV_DIM + ATT_DIM:]))
    hn = _rms(h, nf_ref[...]).astype(BF16)
    d_ff = wd_ref.shape[0]
    for c in range(d_ff // FFN_CHUNK):
        if side_work is not None:
            side_work(c, d_ff // FFN_CHUNK)
        halves = []
        for base in (0, d_ff):
            cols = slice(base + c * FFN_CHUNK, base + (c + 1) * FFN_CHUNK)
            up = _mm(hn, wu_ref[:, cols])
            u1, u2 = shift(up, cols)
            keep(up, cols)
            halves.append(_conv3(up, u1, u2, fcw_ref[:, cols]))
        a, g = halves
        act = g * (1.0 / (1.0 + jnp.exp(-g))) * a
        act_ref[:, c * FFN_CHUNK:(c + 1) * FFN_CHUNK] = act.astype(BF16)
    return h + _mm(act_ref[...], wd_ref[...])


def _page_sums(x):
    return jnp.concatenate([jnp.sum(x[:, h], axis=-1) for h in range(ATT_HEADS)], axis=-1)


def _post_prompt_kernel(*refs, with_pages):
    side_work = None
    if with_pages:
        (x_ref, yc_ref, ya_ref, ym_ref, wo_ref, nf_ref, wu_ref, fcw_ref, wd_ref, ck_ref,
         y_ref, fs_ref, ps_ref, carry_ref, act_ref) = refs

        def side_work(c, n):
            rows = slice(c * SUBLANES, (c + 1) * SUBLANES)
            if rows.stop <= ck_ref.shape[0]:
                ps_ref[rows, :] = _page_sums(ck_ref[rows])
    else:
        (x_ref, yc_ref, ya_ref, ym_ref, wo_ref, nf_ref, wu_ref, fcw_ref, wd_ref,
         y_ref, fs_ref, carry_ref, act_ref) = refs
    tm = x_ref.shape[1]

    @pl.when(pl.program_id(1) == 0)
    def _():
        carry_ref[...] = jnp.zeros_like(carry_ref)

    def keep(up, cols):
        carry_ref[:, cols] = up[tm - SUBLANES:]
        fs_ref[0, :, cols] = up[tm - SUBLANES:]

    shift = lambda up, cols: _shift_tile(up, carry_ref[:, cols])
    y_ref[0] = _post(x_ref[0], yc_ref[0], ya_ref[0], ym_ref[0], wo_ref, nf_ref, wu_ref, fcw_ref, wd_ref,
                     act_ref, shift, keep, side_work)


def _post_prompt(x, yc, ya, ym, w_o, nf, w_up, fcw, w_down, ck_t=None):
    nb, s, d = x.shape
    tm = POST_TILE if s % POST_TILE == 0 else MOBA_BLOCK
    f2 = w_up.shape[1]
    const = lambda a: pl.BlockSpec(a.shape, lambda b, i: (0,) * a.ndim, pipeline_mode=pl.Buffered(1))
    out_shape = [jax.ShapeDtypeStruct((nb, s, d), F32), jax.ShapeDtypeStruct((nb, SUBLANES, f2), F32)]
    args = (x, yc, ya, ym, w_o, nf, w_up, fcw, w_down)
    with_pages = False
    if ck_t is not None:
        n_pool = ck_t.shape[0]
        steps = nb * (s // MOBA_BLOCK)
        pages = n_pool // steps
        room = min(PAGE_SUM_PAGES[0] // 2, SUBLANES * (w_down.shape[0] // FFN_CHUNK))
        with_pages = n_pool % steps == 0 and pages % SUBLANES == 0 and pages <= room
    if with_pages:
        tm = MOBA_BLOCK
    nt = s // tm
    tok = lambda w: pl.BlockSpec((1, tm, w), lambda b, i: (b, i, 0))
    in_specs = [tok(d), tok(CONV_DIM), tok(ATT_DIM), tok(MEM_DIM),
                const(w_o), const(nf), const(w_up), const(fcw), const(w_down)]
    out_specs = [tok(d), pl.BlockSpec((1, SUBLANES, f2), lambda b, i: (b, 0, 0))]
    if with_pages:
        in_specs.append(pl.BlockSpec((pages,) + ck_t.shape[1:], lambda b, i: (b * nt + i, 0, 0, 0)))
        out_specs.append(pl.BlockSpec((pages, ATT_DIM), lambda b, i: (b * nt + i, 0)))
        out_shape.append(jax.ShapeDtypeStruct((n_pool, ATT_DIM), F32))
        args = args + (ck_t,)
    outs = pl.pallas_call(
        functools.partial(_post_prompt_kernel, with_pages=with_pages),
        grid=(nb, nt),
        in_specs=in_specs,
        out_specs=out_specs,
        out_shape=out_shape,
        scratch_shapes=[pltpu.VMEM((SUBLANES, f2), F32), pltpu.VMEM((tm, f2 // 2), BF16)],
        compiler_params=pltpu.CompilerParams(dimension_semantics=("arbitrary", "arbitrary"),
                                             vmem_limit_bytes=VMEM_LIMIT_POST),
        name="post_prompt",
    )(*args)
    if with_pages:
        return outs
    return outs[0], outs[1], (None if ck_t is None else _page_sum(ck_t))


def _inproj_sample_kernel(x_ref, pe_ref, nm_ref, w_ref, cw_ref, qg_ref, kg_ref, mqg_ref, bd_ref,
                          u_ref, yc_ref, q_ref, k_ref, v_ref, mq_ref):
    shift = lambda u: _shift_groups(u, pe_ref[...])
    u, y_conv, q, k, v, mq = _project(x_ref[...], nm_ref, w_ref, cw_ref, qg_ref, kg_ref, mqg_ref, bd_ref, shift)
    u_ref[...] = u
    yc_ref[...] = y_conv.astype(BF16)
    q_ref[...] = q
    k_ref[...] = k
    v_ref[...] = v
    mq_ref[...] = (mq * SCALE).astype(BF16)


def _inproj_sample(x, prev_ext, nm, w_in, cw, qg, kg, mqg, bd):
    n = x.shape[0]
    shp = lambda w, dt: jax.ShapeDtypeStruct((n, w), dt)
    return pl.pallas_call(
        _inproj_sample_kernel,
        out_shape=[shp(CONV_DIM, F32), shp(CONV_DIM, BF16), shp(ATT_DIM, F32), shp(ATT_DIM, F32),
                   shp(ATT_DIM, F32), shp(MEM_DIM, BF16)],
        compiler_params=pltpu.CompilerParams(vmem_limit_bytes=VMEM_LIMIT),
        name="inproj_sample",
    )(x, prev_ext, nm, w_in, cw, qg, kg, mqg, bd)


def _post_sample_kernel(x_ref, yc_ref, ya_ref, ym_ref, pe_ref, wo_ref, nf_ref, wu_ref, fcw_ref, wd_ref,
                        y_ref, up_ref, act_ref):
    def keep(up, cols):
        up_ref[:, cols] = up

    shift = lambda up, cols: _shift_groups(up, pe_ref[:, cols])
    y_ref[...] = _post(x_ref[...], yc_ref[...], ya_ref[...], ym_ref[...], wo_ref, nf_ref, wu_ref, fcw_ref,
                       wd_ref, act_ref, shift, keep)


def _post_sample(x, yc, ya, ym, prev_ext, w_o, nf, w_up, fcw, w_down):
    n, d = x.shape
    return pl.pallas_call(
        _post_sample_kernel,
        out_shape=[jax.ShapeDtypeStruct((n, d), F32), jax.ShapeDtypeStruct((n, w_up.shape[1]), F32)],
        scratch_shapes=[pltpu.VMEM((n, w_down.shape[0]), BF16)],
        compiler_params=pltpu.CompilerParams(vmem_limit_bytes=VMEM_LIMIT),
        name="post_sample",
    )(x, yc, ya, ym, prev_ext, w_o, nf, w_up, fcw, w_down)


def _page_sum_kernel(ck_ref, o_ref):
    o_ref[...] = _page_sums(ck_ref[...])


def _page_sum(ck_t):
    n_pool = ck_t.shape[0]
    pages_per_step = next((p for p in PAGE_SUM_PAGES if n_pool % p == 0), n_pool)
    return pl.pallas_call(
        _page_sum_kernel,
        grid=(n_pool // pages_per_step,),
        in_specs=[pl.BlockSpec((pages_per_step, ATT_HEADS, HEAD_DIM, PAGE_SIZE), lambda i: (i, 0, 0, 0))],
        out_specs=pl.BlockSpec((pages_per_step, ATT_DIM), lambda i: (i, 0)),
        out_shape=jax.ShapeDtypeStruct((n_pool, ATT_DIM), F32),
        compiler_params=pltpu.CompilerParams(dimension_semantics=("arbitrary",), vmem_limit_bytes=VMEM_LIMIT),
        name="page_sum",
    )(ck_t)


def _sample_gate_kernel(pt_ref, q_ref, ps_ref, sel_ref, km_ref):
    b = pl.program_id(0)
    nfp = km_ref.shape[0]
    n_pages = nfp * PAGES_PER_BLOCK

    def gather(n, carry):
        i0 = pt_ref[b * n_pages + PAGES_PER_BLOCK * n]
        i1 = pt_ref[b * n_pages + PAGES_PER_BLOCK * n + 1]
        km_ref[pl.ds(n, 1), :] = (ps_ref[pl.ds(i0, 1), :] + ps_ref[pl.ds(i1, 1), :]) * (1.0 / MOBA_BLOCK)
        return carry

    lax.fori_loop(0, nfp, gather, 0)
    q = q_ref[...]
    km = km_ref[...]
    t = q.shape[0]
    head_of_lane = lax.broadcasted_iota(jnp.int32, q.shape, 1) // HEAD_DIM
    blk = lax.broadcasted_iota(jnp.int32, (t, nfp), 1).astype(F32)
    lane = lax.broadcasted_iota(jnp.int32, (t, LANES), 1)
    out = jnp.zeros((t, LANES), F32)
    for h in range(ATT_HEADS):
        g = _nt3(jnp.where(head_of_lane == h, q, 0.0), km)
        for kk in range(MOBA_TOPK):
            m = jnp.max(g, axis=-1, keepdims=True)
            first = jnp.min(jnp.where(g == m, blk, float(nfp)), axis=-1, keepdims=True)
            out = jnp.where(lane == h * MOBA_TOPK + kk, first, out)
            g = jnp.where(blk == first, -jnp.inf, g)
    sel_ref[...] = out.astype(jnp.int32)


def _sample_gate(pt_flat, q, page_sum, t, nfp):
    n = q.shape[0]
    return pl.pallas_call(
        _sample_gate_kernel,
        grid_spec=pltpu.PrefetchScalarGridSpec(
            num_scalar_prefetch=1,
            grid=(n // t,),
            in_specs=[pl.BlockSpec((t, ATT_DIM), lambda b, pt: (b, 0)),
                      pl.BlockSpec(page_sum.shape, lambda b, pt: (0, 0), pipeline_mode=pl.Buffered(1))],
            out_specs=pl.BlockSpec((t, LANES), lambda b, pt: (b, 0)),
            scratch_shapes=[pltpu.VMEM((nfp, ATT_DIM), F32)]),
        out_shape=jax.ShapeDtypeStruct((n, LANES), jnp.int32),
        compiler_params=pltpu.CompilerParams(dimension_semantics=("arbitrary",), vmem_limit_bytes=VMEM_LIMIT),
        name="sample_gate",
    )(pt_flat, q, page_sum)


def _sample_attn_kernel(pt_ref, sel_ref, slopes_ref, qt_ref, kt_ref, vt_ref, ck_ref, cv_ref, o_ref,
                        kbuf, vbuf, sem, *, n_pages, past_len):
    b = pl.program_id(0)
    h = pl.program_id(1)
    nb = pl.num_programs(0)
    t_len = qt_ref.shape[-1]
    step = b * ATT_HEADS + h
    slot = step % 2
    per_q = MOBA_TOPK * PAGES_PER_BLOCK
    sel_row = ATT_HEADS * MOBA_TOPK

    def copies(bb, hh, sl, t, kk, p):
        blk = sel_ref[(bb * t_len + t) * sel_row + hh * MOBA_TOPK + kk]
        phys = pt_ref[bb * n_pages + blk * PAGES_PER_BLOCK + p]
        idx = t * per_q + kk * PAGES_PER_BLOCK + p
        return (pltpu.make_async_copy(ck_ref.at[phys, hh], kbuf.at[sl, idx], sem.at[sl, 0]),
                pltpu.make_async_copy(cv_ref.at[phys, hh], vbuf.at[sl, idx], sem.at[sl, 1]))

    def for_all(bb, hh, sl, fn):
        def body(t, carry):
            for kk in range(MOBA_TOPK):
                for p in range(PAGES_PER_BLOCK):
                    for cp in copies(bb, hh, sl, t, kk, p):
                        fn(cp)
            return carry
        lax.fori_loop(0, t_len, body, 0)

    @pl.when(step == 0)
    def _():
        for_all(b, h, slot, lambda cp: cp.start())

    @pl.when(step + 1 < nb * ATT_HEADS)
    def _():
        nxt = step + 1
        for_all(nxt // ATT_HEADS, nxt % ATT_HEADS, 1 - slot, lambda cp: cp.start())

    for_all(b, h, slot, lambda cp: cp.wait())

    slope = slopes_ref[h]
    qt = qt_ref[0, 0]
    kt_new = kt_ref[0, 0]
    vt_new = vt_ref[0, 0]
    key = lax.broadcasted_iota(jnp.int32, (1, PAGE_SIZE), 1)
    new = lax.broadcasted_iota(jnp.int32, (1, t_len), 1)
    out_lane = lax.broadcasted_iota(jnp.int32, (HEAD_DIM, t_len), 1)
    out = jnp.zeros((HEAD_DIM, t_len), F32)
    for t in range(t_len):
        qc = qt[:, t:t + 1]
        scores = []
        for kk in range(MOBA_TOPK):
            blk = sel_ref[(b * t_len + t) * sel_row + h * MOBA_TOPK + kk]
            for p in range(PAGES_PER_BLOCK):
                kt = kbuf[slot, t * per_q + kk * PAGES_PER_BLOCK + p]
                dist = (past_len + t - blk * MOBA_BLOCK - p * PAGE_SIZE - key).astype(F32)
                scores.append(jnp.sum(kt * qc, axis=0, keepdims=True) * SCALE - slope * dist)
        s_new = jnp.sum(kt_new * qc, axis=0, keepdims=True) * SCALE - slope * (t - new).astype(F32)
        s_new = jnp.where(new <= t, s_new, NEG)
        m = jnp.maximum(jnp.max(functools.reduce(jnp.maximum, scores), axis=-1, keepdims=True),
                        jnp.max(s_new, axis=-1, keepdims=True))
        p_new = jnp.exp(s_new - m)
        acc = jnp.zeros((HEAD_DIM, PAGE_SIZE), F32)
        p_sum = jnp.zeros((1, PAGE_SIZE), F32)
        for c, s in enumerate(scores):
            pr = jnp.exp(s - m)
            p_sum = p_sum + pr
            acc = acc + vbuf[slot, t * per_q + c] * pr
        denom = jnp.sum(p_sum, axis=-1, keepdims=True) + jnp.sum(p_new, axis=-1, keepdims=True)
        o = jnp.sum(acc, axis=-1, keepdims=True) + jnp.sum(vt_new * p_new, axis=-1, keepdims=True)
        out = jnp.where(out_lane == t, o * (1.0 / denom), out)
    o_ref[0, 0] = out


def _sample_attn(pt_flat, sel_flat, slopes, qt, kt, vt, ck_t, cv_t, n_pages):
    nheads, nb, _, t_len = qt.shape
    per_step = t_len * MOBA_TOPK * PAGES_PER_BLOCK
    tok = pl.BlockSpec((1, 1, HEAD_DIM, t_len), lambda b, h, pt, sel: (h, b, 0, 0))
    kernel = functools.partial(_sample_attn_kernel, n_pages=n_pages, past_len=n_pages * PAGE_SIZE)
    return pl.pallas_call(
        kernel,
        grid_spec=pltpu.PrefetchScalarGridSpec(
            num_scalar_prefetch=2,
            grid=(nb, nheads),
            in_specs=[pl.BlockSpec(memory_space=pltpu.SMEM), tok, tok, tok,
                      pl.BlockSpec(memory_space=pl.ANY), pl.BlockSpec(memory_space=pl.ANY)],
            out_specs=tok,
            scratch_shapes=[pltpu.VMEM((2, per_step, HEAD_DIM, PAGE_SIZE), F32),
                            pltpu.VMEM((2, per_step, HEAD_DIM, PAGE_SIZE), F32),
                            pltpu.SemaphoreType.DMA((2, 2))]),
        out_shape=jax.ShapeDtypeStruct((nheads, nb, HEAD_DIM, t_len), F32),
        compiler_params=pltpu.CompilerParams(dimension_semantics=("arbitrary", "arbitrary"),
                                             vmem_limit_bytes=VMEM_LIMIT),
        name="sample_attention",
    )(pt_flat, sel_flat, slopes, qt, kt, vt, ck_t, cv_t)


def _prev_ext(cache, t_len):
    nb, w, c = cache.shape
    return jnp.concatenate([cache, jnp.zeros((nb, t_len - w, c), cache.dtype)], axis=1).reshape(nb * t_len, c)


def kernel(x_prompt, x_sample, mem_prompt, cache_k, cache_v, page_table, cache_conv, cache_ffn_conv, cache_mem_k, cache_mem_v, norm_mix, w_in, conv_w, q_norm, k_norm, memq_norm, mem_norm, w_mem_kv, memk_norm, w_o, norm_ffn, w_up, ffn_conv_w, w_down):
    depth = w_in.shape[0]
    assert depth == 1
    nbp, seq, d_model = x_prompt.shape
    nbs, t_len, _ = x_sample.shape
    n_pages = page_table.shape[1]
    assert t_len == SUBLANES and n_pages % PAGES_PER_BLOCK == 0
    nfp = n_pages // PAGES_PER_BLOCK

    slopes = jnp.exp2(-8.0 * jnp.arange(1, ATT_HEADS + 1, dtype=F32) / ATT_HEADS)
    bd = (jnp.kron(jnp.eye(ATT_HEADS, dtype=F32), jnp.ones((HEAD_DIM, HEAD_DIM), F32)) / HEAD_DIM).astype(BF16)
    nm, nf, mn = norm_mix, norm_ffn, mem_norm
    qg = jnp.tile(q_norm, (1, ATT_HEADS))
    kg = jnp.tile(k_norm, (1, ATT_HEADS))
    mqg = jnp.tile(memq_norm, (1, MEM_HEADS))
    mkg = jnp.tile(memk_norm, (1, MEM_HEADS))
    w_in_b = w_in[0].astype(BF16)
    w_o_b = w_o[0].astype(BF16)
    w_up_b = w_up[0].astype(BF16)
    w_down_b = w_down[0].astype(BF16)
    w_mem_b = w_mem_kv[0].astype(BF16)
    cw, fcw = conv_w[0], ffn_conv_w[0]

    mk_t, mv_t = _memkv(mem_prompt, mn, w_mem_b, mkg, bd[:MEM_DIM, :MEM_DIM])
    mem_len = mk_t.shape[-1]
    yc, k_p, v_p, mq, q_aug, k_aug, v_aug, cs_p = _inproj_prompt(slopes, x_prompt, nm, w_in_b, cw, qg, kg, mqg, bd)
    ya = _moba_prompt(q_aug, k_aug, v_aug)
    ym = _memattn(mq, mk_t, mv_t, 2 * MOBA_BLOCK)
    ck_t = jnp.transpose(cache_k[0], (0, 2, 3, 1))
    cv_t = jnp.transpose(cache_v[0], (0, 2, 3, 1))
    y_p, fs_p, page_sum = _post_prompt(x_prompt, yc, ya, ym, w_o_b, nf, w_up_b, fcw, w_down_b, ck_t)

    n_s = nbs * t_len
    xs = x_sample.reshape(n_s, d_model)
    u_s, yc_s, q_s, k_s, v_s, mq_s = _inproj_sample(xs, _prev_ext(cache_conv[0], t_len), nm, w_in[0], cw,
                                                    qg, kg, mqg, bd)
    pt_flat = page_table.reshape(-1)
    sel = _sample_gate(pt_flat, q_s, page_sum, t_len, nfp)
    sel_flat = sel[:, :ATT_HEADS * MOBA_TOPK].reshape(-1)
    heads_t = lambda a: jnp.transpose(a.reshape(nbs, t_len, ATT_HEADS, HEAD_DIM), (2, 0, 3, 1))
    ya_t = _sample_attn(pt_flat, sel_flat, slopes, heads_t(q_s), heads_t(k_s), heads_t(v_s), ck_t, cv_t, n_pages)
    ya_s = jnp.transpose(ya_t, (1, 3, 0, 2)).reshape(n_s, ATT_DIM).astype(BF16)
    cmk_t = jnp.transpose(cache_mem_k[0], (0, 2, 3, 1)).reshape(nbs, MEM_DIM, mem_len)
    cmv_t = jnp.transpose(cache_mem_v[0], (0, 2, 3, 1)).reshape(nbs, MEM_DIM, mem_len)
    ym_s = _memattn(mq_s.reshape(nbs, t_len, MEM_DIM), cmk_t, cmv_t, t_len).reshape(n_s, MEM_DIM)
    y_s, up_s = _post_sample(xs, yc_s, ya_s, ym_s, _prev_ext(cache_ffn_conv[0], t_len), w_o_b, nf, w_up_b,
                             fcw, w_down_b)

    heads = lambda a, nb, t: a.reshape(1, nb, t, ATT_HEADS, HEAD_DIM)
    tail = lambda a, nb, t: a.reshape(nb, t, -1)[None, :, t - 2:]
    mem_out = lambda a: jnp.transpose(a.reshape(nbp, MEM_HEADS, HEAD_DIM, mem_len), (0, 3, 1, 2))[None]
    return (y_p, y_s.reshape(nbs, t_len, d_model),
            heads(k_p, nbp, seq), heads(v_p, nbp, seq), heads(k_s, nbs, t_len), heads(v_s, nbs, t_len),
            cs_p[None, :, SUBLANES - 2:], tail(u_s, nbs, t_len),
            fs_p[None, :, SUBLANES - 2:], tail(up_s, nbs, t_len),
            mem_out(mk_t), mem_out(mv_t))
```

```python
import functools

import jax
import jax.numpy as jnp
from jax import lax
from jax.experimental import pallas as pl
from jax.experimental.pallas import tpu as pltpu

HEAD_DIM = 64
CONV_DIM = 384
ATT_HEADS = 6
ATT_DIM = ATT_HEADS * HEAD_DIM
MEM_HEADS = 4
MEM_DIM = MEM_HEADS * HEAD_DIM
MOBA_BLOCK = 256
MOBA_TOPK = 3
PAGE_SIZE = 128
PAGES_PER_BLOCK = MOBA_BLOCK // PAGE_SIZE
EPS = 1e-6
SCALE = HEAD_DIM ** -0.5
LOG2E = 1.4426950408889634

LANES = 128
SUBLANES = 8
NEG = -1e30
SEL_LANE0 = HEAD_DIM
ALIBI_LANE0 = SEL_LANE0 + 32
MAX_BLOCKS = ALIBI_LANE0 - SEL_LANE0
FFN_CHUNK = 256
KV_GROUP = 8
POST_TILE = 512
PAGE_SUM_PAGES = (80, 64, 40, 32, 16, 8)
VMEM_LIMIT = 48 * 1024 * 1024
VMEM_LIMIT_POST = 56 * 1024 * 1024

F32 = jnp.float32
BF16 = jnp.bfloat16


def _nt(a, b):
    return lax.dot_general(a, b, (((1,), (1,)), ((), ())), preferred_element_type=F32)


def _mm(a, b):
    return jnp.dot(a, b, preferred_element_type=F32)


def _split(x):
    hi = x.astype(BF16)
    lo = (x - hi.astype(F32)).astype(BF16)
    return hi, lo


def _nt3(a, b):
    a_hi, a_lo = _split(a)
    b_hi, b_lo = _split(b)
    return _nt(a_hi, b_hi) + _nt(a_hi, b_lo) + _nt(a_lo, b_hi)


def _rms(x, g):
    return x * lax.rsqrt(jnp.mean(x * x, axis=-1, keepdims=True) + EPS) * g


def _head_rms(x, bd, g, split=True):
    hi, lo = _split(x * x)
    ms = _mm(hi, bd) + _mm(lo, bd) if split else _mm(hi, bd)
    return x * lax.rsqrt(ms + EPS) * g


def _conv3(u, u1, u2, w):
    return w[0:1] * u2 + w[1:2] * u1 + w[2:3] * u


def _shift_tile(u, prev8):
    row = lax.broadcasted_iota(jnp.int32, u.shape, 0)
    p6 = jnp.broadcast_to(prev8[6:7], u.shape)
    p7 = jnp.broadcast_to(prev8[7:8], u.shape)
    u1 = jnp.where(row == 0, p7, pltpu.roll(u, 1, 0))
    u2 = jnp.where(row == 0, p6, jnp.where(row == 1, p7, pltpu.roll(u, 2, 0)))
    return u1, u2


def _shift_groups(u, prev_ext):
    n = u.shape[0]
    r8 = lax.broadcasted_iota(jnp.int32, u.shape, 0) % SUBLANES
    u1 = jnp.where(r8 == 0, pltpu.roll(prev_ext, n - 1, 0), pltpu.roll(u, 1, 0))
    u2 = jnp.where(r8 < 2, prev_ext, pltpu.roll(u, 2, 0))
    return u1, u2


def _project(x, nm_ref, w_ref, cw_ref, qg_ref, kg_ref, mqg_ref, bd_ref, shift):
    xn, xn_lo = _split(_rms(x, nm_ref[...]))
    if w_ref.dtype == BF16:
        seg = lambda a, b: _mm(xn, w_ref[:, a:b])
    else:
        def seg(a, b):
            w_hi, w_lo = _split(w_ref[:, a:b])
            return _mm(xn, w_hi) + _mm(xn, w_lo) + _mm(xn_lo, w_hi)
    c0, c1, c2 = CONV_DIM, 2 * CONV_DIM, 3 * CONV_DIM
    a1, a2, a3 = c2 + ATT_DIM, c2 + 2 * ATT_DIM, c2 + 3 * ATT_DIM
    u = seg(c0, c1) * seg(c1, c2)
    u1, u2 = shift(u)
    y_conv = seg(0, c0) * _conv3(u, u1, u2, cw_ref[...])
    bd = bd_ref[...]
    split = w_ref.dtype != BF16
    q = _head_rms(seg(c2, a1), bd, qg_ref[...], split)
    k = _head_rms(seg(a1, a2), bd, kg_ref[...], split)
    v = seg(a2, a3)
    mq = _head_rms(seg(a3, a3 + MEM_DIM), bd[:MEM_DIM, :MEM_DIM], mqg_ref[...], split)
    return u, y_conv, q, k, v, mq


def _inproj_prompt_kernel(slopes_ref, x_ref, nm_ref, w_ref, cw_ref, qg_ref, kg_ref, mqg_ref, bd_ref,
                          yc_ref, k_ref, v_ref, mq_ref, qa_ref, ka_ref, va_ref, cs_ref,
                          carry_ref, km_ref):
    i = pl.program_id(1)
    tm = x_ref.shape[1]

    @pl.when(i == 0)
    def _():
        carry_ref[...] = jnp.zeros_like(carry_ref)
        km_ref[...] = jnp.zeros_like(km_ref)

    shift = lambda u: _shift_tile(u, carry_ref[...])
    u, y_conv, q, k, v, mq = _project(x_ref[0], nm_ref, w_ref, cw_ref, qg_ref, kg_ref, mqg_ref, bd_ref, shift)
    carry_ref[...] = u[tm - SUBLANES:]
    cs_ref[0] = u[tm - SUBLANES:]
    yc_ref[0] = y_conv.astype(BF16)
    k_ref[0] = k
    v_ref[0] = v
    mq_ref[0] = (mq * SCALE).astype(BF16)

    gate = _nt3(q, km_ref[...])
    lane = lax.broadcasted_iota(jnp.int32, (tm, LANES), 1)
    lane_f = lane.astype(F32)
    row = lax.broadcasted_iota(jnp.int32, (tm, LANES), 0)
    pos = (i * tm + row).astype(F32)
    own = lane == SEL_LANE0 + i
    past = (lane >= SEL_LANE0) & (lane < SEL_LANE0 + i)
    for h in range(ATT_HEADS):
        g = jnp.where(past, gate[:, h * LANES:(h + 1) * LANES], -jnp.inf)
        sel = jnp.zeros(own.shape, jnp.bool_)
        for _ in range(MOBA_TOPK):
            m = jnp.max(g, axis=-1, keepdims=True)
            first = jnp.min(jnp.where((g == m) & past, lane_f, float(LANES)), axis=-1, keepdims=True)
            pick = (lane_f == first) & (m > -jnp.inf)
            sel = sel | pick
            g = jnp.where(pick, -jnp.inf, g)
        sel_bias = jnp.where(sel, 0.0, NEG)

        pair = slice((h // 2) * LANES, (h // 2 + 1) * LANES)
        head = (lambda t: t[:, pair]) if h % 2 == 0 else (lambda t: pltpu.roll(t[:, pair], HEAD_DIM, 1))
        low = lane < HEAD_DIM
        q_aux = jnp.where(lane < ALIBI_LANE0, sel_bias, jnp.where(lane < ALIBI_LANE0 + 3, 1.0, 0.0))
        qa_ref[0, h] = jnp.where(low, head(q) * (SCALE * LOG2E), q_aux).astype(BF16)

        a = (slopes_ref[h] * LOG2E) * pos
        a_hi = a.astype(BF16).astype(F32)
        a_mid = (a - a_hi).astype(BF16).astype(F32)
        a_lo = a - a_hi - a_mid
        k_aux = jnp.where(own, 1.0, 0.0)
        k_aux = jnp.where(lane == ALIBI_LANE0, a_hi, k_aux)
        k_aux = jnp.where(lane == ALIBI_LANE0 + 1, a_mid, k_aux)
        k_aux = jnp.where(lane == ALIBI_LANE0 + 2, a_lo, k_aux)
        ka_ref[0, h] = jnp.where(low, head(k), k_aux).astype(BF16)
        va_ref[0, h] = jnp.where(low, head(v), jnp.where(lane == HEAD_DIM, 1.0, 0.0)).astype(BF16)

    kmean = jnp.sum(k, axis=0, keepdims=True) * (1.0 / MOBA_BLOCK)
    head_of_lane = lax.broadcasted_iota(jnp.int32, kmean.shape, 1) // HEAD_DIM
    for h in range(ATT_HEADS):
        km_ref[pl.ds(h * LANES + SEL_LANE0 + i, 1), :] = jnp.where(head_of_lane == h, kmean, 0.0)


def _inproj_prompt(slopes, x, nm, w_in, cw, qg, kg, mqg, bd):
    nb, s, d = x.shape
    tm = MOBA_BLOCK
    nt = s // tm
    assert s % tm == 0 and nt <= MAX_BLOCKS
    const = lambda shape: pl.BlockSpec(shape, lambda b, i: (0,) * len(shape), pipeline_mode=pl.Buffered(1))
    tok = lambda w: pl.BlockSpec((1, tm, w), lambda b, i: (b, i, 0))
    aug = pl.BlockSpec((1, ATT_HEADS, tm, LANES), lambda b, i: (b, 0, i, 0))
    aug_shape = jax.ShapeDtypeStruct((nb, ATT_HEADS, s, LANES), BF16)
    return pl.pallas_call(
        _inproj_prompt_kernel,
        grid=(nb, nt),
        in_specs=[pl.BlockSpec(memory_space=pltpu.SMEM), tok(d), const(nm.shape), const(w_in.shape),
                  const(cw.shape), const(qg.shape), const(kg.shape), const(mqg.shape), const(bd.shape)],
        out_specs=[tok(CONV_DIM), tok(ATT_DIM), tok(ATT_DIM), tok(MEM_DIM), aug, aug, aug,
                   pl.BlockSpec((1, SUBLANES, CONV_DIM), lambda b, i: (b, 0, 0))],
        out_shape=[jax.ShapeDtypeStruct((nb, s, CONV_DIM), BF16),
                   jax.ShapeDtypeStruct((nb, s, ATT_DIM), F32),
                   jax.ShapeDtypeStruct((nb, s, ATT_DIM), F32),
                   jax.ShapeDtypeStruct((nb, s, MEM_DIM), BF16),
                   aug_shape, aug_shape, aug_shape,
                   jax.ShapeDtypeStruct((nb, SUBLANES, CONV_DIM), F32)],
        scratch_shapes=[pltpu.VMEM((SUBLANES, CONV_DIM), F32),
                        pltpu.VMEM((ATT_HEADS * LANES, ATT_DIM), F32)],
        compiler_params=pltpu.CompilerParams(dimension_semantics=("arbitrary", "arbitrary"),
                                             vmem_limit_bytes=VMEM_LIMIT),
        name="inproj_prompt",
    )(slopes, x, nm, w_in, cw, qg, kg, mqg, bd)


def _moba_prompt_kernel(qa_ref, ka_ref, va_ref, o_ref, s_ref, mg_ref, m_ref, acc_ref):
    i = pl.program_id(2)
    tq = qa_ref.shape[2]
    tg = s_ref.shape[-1]
    n_groups = (i * tq + tg - 1) // tg
    qs = [qa_ref[0, hh] for hh in range(2)]
    keys = lambda ref, hh, g: ref[0, hh, pl.ds(pl.multiple_of(g * tg, tg), tg), :]
    lane = lax.broadcasted_iota(jnp.int32, (tq, LANES), 1)

    def score(g, slot):
        for hh in range(2):
            s = _nt(qs[hh], keys(ka_ref, hh, g))
            s_ref[slot, hh] = s
            mg_ref[slot, hh] = jnp.max(s, axis=-1, keepdims=True)

    def weigh(g, slot):
        for hh in range(2):
            m_old = m_ref[hh]
            m_new = jnp.maximum(m_old, mg_ref[slot, hh])
            p = jnp.exp2(s_ref[slot, hh] - m_new).astype(BF16)
            acc_ref[hh] = jnp.exp2(m_old - m_new) * acc_ref[hh] + _mm(p, keys(va_ref, hh, g))
            m_ref[hh] = m_new

    score(0, 0)
    own = pl.ds(pl.multiple_of(i * tq, tq), tq)
    causal = lax.broadcasted_iota(jnp.int32, (tq, tq), 0) >= lax.broadcasted_iota(jnp.int32, (tq, tq), 1)
    sel_lanes = (lane >= SEL_LANE0) & (lane < ALIBI_LANE0)
    for hh in range(2):
        k_own = ka_ref[0, hh, own, :]
        k_own = jnp.where(sel_lanes, jnp.zeros_like(k_own), k_own)
        s = jnp.where(causal, _nt(qs[hh], k_own), NEG)
        m = jnp.max(s, axis=-1, keepdims=True)
        m_ref[hh] = m
        acc_ref[hh] = _mm(jnp.exp2(s - m).astype(BF16), va_ref[0, hh, own, :])

    def trip(g, carry):
        weigh(g, g % 2)
        score(g + 1, (g + 1) % 2)
        return carry

    lax.fori_loop(0, n_groups - 1, trip, 0)
    last = jnp.maximum(n_groups - 1, 0)
    weigh(last, last % 2)
    outs = [acc_ref[hh] * (1.0 / acc_ref[hh][:, HEAD_DIM:HEAD_DIM + 1]) for hh in range(2)]
    o_ref[0] = jnp.where(lane < HEAD_DIM, outs[0], pltpu.roll(outs[1], HEAD_DIM, 1)).astype(BF16)


def _moba_prompt(q_aug, k_aug, v_aug):
    nb, _, s, _ = q_aug.shape
    tq = MOBA_BLOCK
    tg = min(KV_GROUP * MOBA_BLOCK, s)
    assert s % tg == 0
    kv = pl.BlockSpec((1, 2, s, LANES), lambda b, p, i: (b, p, 0, 0))
    return pl.pallas_call(
        _moba_prompt_kernel,
        grid=(nb, ATT_HEADS // 2, s // tq),
        in_specs=[pl.BlockSpec((1, 2, tq, LANES), lambda b, p, i: (b, p, i, 0)), kv, kv],
        out_specs=pl.BlockSpec((1, tq, LANES), lambda b, p, i: (b, i, p)),
        out_shape=jax.ShapeDtypeStruct((nb, s, ATT_DIM), BF16),
        scratch_shapes=[pltpu.VMEM((2, 2, tq, tg), F32), pltpu.VMEM((2, 2, tq, 1), F32),
                        pltpu.VMEM((2, tq, 1), F32), pltpu.VMEM((2, tq, LANES), F32)],
        compiler_params=pltpu.CompilerParams(dimension_semantics=("arbitrary", "arbitrary", "arbitrary"),
                                             vmem_limit_bytes=VMEM_LIMIT),
        name="moba_prompt",
    )(q_aug, k_aug, v_aug)


def _memkv_kernel(mem_ref, g_ref, w_ref, kg_ref, bd_ref, mk_ref, mv_ref):
    kv = _mm(_rms(mem_ref[0], g_ref[...]).astype(BF16), w_ref[...])
    mk = _head_rms(kv[:, :MEM_DIM], bd_ref[...], kg_ref[...])
    mk_ref[0] = mk.T
    mv_ref[0] = kv[:, MEM_DIM:].T


def _memkv(mem, g, w, kg, bd):
    nb, m, d = mem.shape
    const = lambda a: pl.BlockSpec(a.shape, lambda b: (0,) * a.ndim)
    out = pl.BlockSpec((1, MEM_DIM, m), lambda b: (b, 0, 0))
    return pl.pallas_call(
        _memkv_kernel,
        grid=(nb,),
        in_specs=[pl.BlockSpec((1, m, d), lambda b: (b, 0, 0)), const(g), const(w), const(kg), const(bd)],
        out_specs=[out, out],
        out_shape=[jax.ShapeDtypeStruct((nb, MEM_DIM, m), F32)] * 2,
        compiler_params=pltpu.CompilerParams(dimension_semantics=("arbitrary",)),
        name="memory_kv",
    )(mem, g, w, kg, bd)


def _memattn_kernel(mq_ref, mk_ref, mv_ref, o_ref):
    mq = mq_ref[0]
    mk = mk_ref[0].astype(BF16)
    mv = mv_ref[0].astype(BF16)
    head_of_lane = lax.broadcasted_iota(jnp.int32, mq.shape, 1) // HEAD_DIM
    out = jnp.zeros(mq.shape, F32)
    for h in range(MEM_HEADS):
        mine = head_of_lane == h
        s = _mm(jnp.where(mine, mq, jnp.zeros_like(mq)), mk)
        p = jnp.exp(s - jnp.max(s, axis=-1, keepdims=True))
        inv = 1.0 / jnp.sum(p, axis=-1, keepdims=True)
        out = jnp.where(mine, _nt(p.astype(BF16), mv) * inv, out)
    o_ref[0] = out.astype(BF16)


def _memattn(mq, mk_t, mv_t, tq):
    nb, t, _ = mq.shape
    m = mk_t.shape[-1]
    kv = pl.BlockSpec((1, MEM_DIM, m), lambda b, i: (b, 0, 0))
    return pl.pallas_call(
        _memattn_kernel,
        grid=(nb, t // tq),
        in_specs=[pl.BlockSpec((1, tq, MEM_DIM), lambda b, i: (b, i, 0)), kv, kv],
        out_specs=pl.BlockSpec((1, tq, MEM_DIM), lambda b, i: (b, i, 0)),
        out_shape=jax.ShapeDtypeStruct((nb, t, MEM_DIM), BF16),
        compiler_params=pltpu.CompilerParams(dimension_semantics=("arbitrary", "arbitrary")),
        name="memory_attention",
    )(mq, mk_t, mv_t)


def _post(x, yc, ya, ym, wo_ref, nf_ref, wu_ref, fcw_ref, wd_ref, act_ref, shift, keep, side_work=None):
    h = x + _mm(jnp.concatenate([yc, ya, ym], axis=-1), wo_ref[...])
    hn = _rms(h, nf_ref[...]).astype(BF16)
    d_ff = wd_ref.shape[0]
    for c in range(d_ff // FFN_CHUNK):
        if side_work is not None:
            side_work(c, d_ff // FFN_CHUNK)
        halves = []
        for base in (0, d_ff):
            cols = slice(base + c * FFN_CHUNK, base + (c + 1) * FFN_CHUNK)
            up = _mm(hn, wu_ref[:, cols])
            u1, u2 = shift(up, cols)
            keep(up, cols)
            halves.append(_conv3(up, u1, u2, fcw_ref[:, cols]))
        a, g = halves
        act = g * (1.0 / (1.0 + jnp.exp(-g))) * a
        act_ref[:, c * FFN_CHUNK:(c + 1) * FFN_CHUNK] = act.astype(BF16)
    return h + _mm(act_ref[...], wd_ref[...])


def _page_sums(x):
    return jnp.concatenate([jnp.sum(x[:, h], axis=-1) for h in range(ATT_HEADS)], axis=-1)


def _post_prompt_kernel(*refs, with_pages):
    side_work = None
    if with_pages:
        (x_ref, yc_ref, ya_ref, ym_ref, wo_ref, nf_ref, wu_ref, fcw_ref, wd_ref, ck_ref,
         y_ref, fs_ref, ps_ref, carry_ref, act_ref) = refs

        def side_work(c, n):
            rows = slice(c * SUBLANES, (c + 1) * SUBLANES)
            if rows.stop <= ck_ref.shape[0]:
                ps_ref[rows, :] = _page_sums(ck_ref[rows])
    else:
        (x_ref, yc_ref, ya_ref, ym_ref, wo_ref, nf_ref, wu_ref, fcw_ref, wd_ref,
         y_ref, fs_ref, carry_ref, act_ref) = refs
    tm = x_ref.shape[1]

    @pl.when(pl.program_id(1) == 0)
    def _():
        carry_ref[...] = jnp.zeros_like(carry_ref)

    def keep(up, cols):
        carry_ref[:, cols] = up[tm - SUBLANES:]
        fs_ref[0, :, cols] = up[tm - SUBLANES:]

    shift = lambda up, cols: _shift_tile(up, carry_ref[:, cols])
    y_ref[0] = _post(x_ref[0], yc_ref[0], ya_ref[0], ym_ref[0], wo_ref, nf_ref, wu_ref, fcw_ref, wd_ref,
                     act_ref, shift, keep, side_work)


def _post_prompt(x, yc, ya, ym, w_o, nf, w_up, fcw, w_down, ck_t=None):
    nb, s, d = x.shape
    tm = POST_TILE if s % POST_TILE == 0 else MOBA_BLOCK
    f2 = w_up.shape[1]
    const = lambda a: pl.BlockSpec(a.shape, lambda b, i: (0,) * a.ndim, pipeline_mode=pl.Buffered(1))
    out_shape = [jax.ShapeDtypeStruct((nb, s, d), F32), jax.ShapeDtypeStruct((nb, SUBLANES, f2), F32)]
    args = (x, yc, ya, ym, w_o, nf, w_up, fcw, w_down)
    with_pages = False
    if ck_t is not None:
        n_pool = ck_t.shape[0]
        steps = nb * (s // MOBA_BLOCK)
        pages = n_pool // steps
        room = min(PAGE_SUM_PAGES[0] // 2, SUBLANES * (w_down.shape[0] // FFN_CHUNK))
        with_pages = n_pool % steps == 0 and pages % SUBLANES == 0 and pages <= room
    if with_pages:
        tm = MOBA_BLOCK
    nt = s // tm
    tok = lambda w: pl.BlockSpec((1, tm, w), lambda b, i: (b, i, 0))
    in_specs = [tok(d), tok(CONV_DIM), tok(ATT_DIM), tok(MEM_DIM),
                const(w_o), const(nf), const(w_up), const(fcw), const(w_down)]
    out_specs = [tok(d), pl.BlockSpec((1, SUBLANES, f2), lambda b, i: (b, 0, 0))]
    if with_pages:
        in_specs.append(pl.BlockSpec((pages,) + ck_t.shape[1:], lambda b, i: (b * nt + i, 0, 0, 0)))
        out_specs.append(pl.BlockSpec((pages, ATT_DIM), lambda b, i: (b * nt + i, 0)))
        out_shape.append(jax.ShapeDtypeStruct((n_pool, ATT_DIM), F32))
        args = args + (ck_t,)
    outs = pl.pallas_call(
        functools.partial(_post_prompt_kernel, with_pages=with_pages),
        grid=(nb, nt),
        in_specs=in_specs,
        out_specs=out_specs,
        out_shape=out_shape,
        scratch_shapes=[pltpu.VMEM((SUBLANES, f2), F32), pltpu.VMEM((tm, f2 // 2), BF16)],
        compiler_params=pltpu.CompilerParams(dimension_semantics=("arbitrary", "arbitrary"),
                                             vmem_limit_bytes=VMEM_LIMIT_POST),
        name="post_prompt",
    )(*args)
    if with_pages:
        return outs
    return outs[0], outs[1], (None if ck_t is None else _page_sum(ck_t))


def _inproj_sample_kernel(x_ref, pe_ref, nm_ref, w_ref, cw_ref, qg_ref, kg_ref, mqg_ref, bd_ref,
                          u_ref, yc_ref, q_ref, k_ref, v_ref, mq_ref):
    shift = lambda u: _shift_groups(u, pe_ref[...])
    u, y_conv, q, k, v, mq = _project(x_ref[...], nm_ref, w_ref, cw_ref, qg_ref, kg_ref, mqg_ref, bd_ref, shift)
    u_ref[...] = u
    yc_ref[...] = y_conv.astype(BF16)
    q_ref[...] = q
    k_ref[...] = k
    v_ref[...] = v
    mq_ref[...] = (mq * SCALE).astype(BF16)


def _inproj_sample(x, prev_ext, nm, w_in, cw, qg, kg, mqg, bd):
    n = x.shape[0]
    shp = lambda w, dt: jax.ShapeDtypeStruct((n, w), dt)
    return pl.pallas_call(
        _inproj_sample_kernel,
        out_shape=[shp(CONV_DIM, F32), shp(CONV_DIM, BF16), shp(ATT_DIM, F32), shp(ATT_DIM, F32),
                   shp(ATT_DIM, F32), shp(MEM_DIM, BF16)],
        compiler_params=pltpu.CompilerParams(vmem_limit_bytes=VMEM_LIMIT),
        name="inproj_sample",
    )(x, prev_ext, nm, w_in, cw, qg, kg, mqg, bd)


def _post_sample_kernel(x_ref, yc_ref, ya_ref, ym_ref, pe_ref, wo_ref, nf_ref, wu_ref, fcw_ref, wd_ref,
                        y_ref, up_ref, act_ref):
    def keep(up, cols):
        up_ref[:, cols] = up

    shift = lambda up, cols: _shift_groups(up, pe_ref[:, cols])
    y_ref[...] = _post(x_ref[...], yc_ref[...], ya_ref[...], ym_ref[...], wo_ref, nf_ref, wu_ref, fcw_ref,
                       wd_ref, act_ref, shift, keep)


def _post_sample(x, yc, ya, ym, prev_ext, w_o, nf, w_up, fcw, w_down):
    n, d = x.shape
    return pl.pallas_call(
        _post_sample_kernel,
        out_shape=[jax.ShapeDtypeStruct((n, d), F32), jax.ShapeDtypeStruct((n, w_up.shape[1]), F32)],
        scratch_shapes=[pltpu.VMEM((n, w_down.shape[0]), BF16)],
        compiler_params=pltpu.CompilerParams(vmem_limit_bytes=VMEM_LIMIT),
        name="post_sample",
    )(x, yc, ya, ym, prev_ext, w_o, nf, w_up, fcw, w_down)


def _page_sum_kernel(ck_ref, o_ref):
    o_ref[...] = _page_sums(ck_ref[...])


def _page_sum(ck_t):
    n_pool = ck_t.shape[0]
    pages_per_step = next((p for p in PAGE_SUM_PAGES if n_pool % p == 0), n_pool)
    return pl.pallas_call(
        _page_sum_kernel,
        grid=(n_pool // pages_per_step,),
        in_specs=[pl.BlockSpec((pages_per_step, ATT_HEADS, HEAD_DIM, PAGE_SIZE), lambda i: (i, 0, 0, 0))],
        out_specs=pl.BlockSpec((pages_per_step, ATT_DIM), lambda i: (i, 0)),
        out_shape=jax.ShapeDtypeStruct((n_pool, ATT_DIM), F32),
        compiler_params=pltpu.CompilerParams(dimension_semantics=("arbitrary",), vmem_limit_bytes=VMEM_LIMIT),
        name="page_sum",
    )(ck_t)


def _sample_gate_kernel(pt_ref, q_ref, ps_ref, sel_ref, km_ref):
    b = pl.program_id(0)
    nfp = km_ref.shape[0]
    n_pages = nfp * PAGES_PER_BLOCK

    def gather(n, carry):
        i0 = pt_ref[b * n_pages + PAGES_PER_BLOCK * n]
        i1 = pt_ref[b * n_pages + PAGES_PER_BLOCK * n + 1]
        km_ref[pl.ds(n, 1), :] = (ps_ref[pl.ds(i0, 1), :] + ps_ref[pl.ds(i1, 1), :]) * (1.0 / MOBA_BLOCK)
        return carry

    lax.fori_loop(0, nfp, gather, 0)
    q = q_ref[...]
    km = km_ref[...]
    t = q.shape[0]
    head_of_lane = lax.broadcasted_iota(jnp.int32, q.shape, 1) // HEAD_DIM
    blk = lax.broadcasted_iota(jnp.int32, (t, nfp), 1).astype(F32)
    lane = lax.broadcasted_iota(jnp.int32, (t, LANES), 1)
    out = jnp.zeros((t, LANES), F32)
    for h in range(ATT_HEADS):
        g = _nt3(jnp.where(head_of_lane == h, q, 0.0), km)
        for kk in range(MOBA_TOPK):
            m = jnp.max(g, axis=-1, keepdims=True)
            first = jnp.min(jnp.where(g == m, blk, float(nfp)), axis=-1, keepdims=True)
            out = jnp.where(lane == h * MOBA_TOPK + kk, first, out)
            g = jnp.where(blk == first, -jnp.inf, g)
    sel_ref[...] = out.astype(jnp.int32)


def _sample_gate(pt_flat, q, page_sum, t, nfp):
    n = q.shape[0]
    return pl.pallas_call(
        _sample_gate_kernel,
        grid_spec=pltpu.PrefetchScalarGridSpec(
            num_scalar_prefetch=1,
            grid=(n // t,),
            in_specs=[pl.BlockSpec((t, ATT_DIM), lambda b, pt: (b, 0)),
                      pl.BlockSpec(page_sum.shape, lambda b, pt: (0, 0), pipeline_mode=pl.Buffered(1))],
            out_specs=pl.BlockSpec((t, LANES), lambda b, pt: (b, 0)),
            scratch_shapes=[pltpu.VMEM((nfp, ATT_DIM), F32)]),
        out_shape=jax.ShapeDtypeStruct((n, LANES), jnp.int32),
        compiler_params=pltpu.CompilerParams(dimension_semantics=("arbitrary",), vmem_limit_bytes=VMEM_LIMIT),
        name="sample_gate",
    )(pt_flat, q, page_sum)


def _sample_attn_kernel(pt_ref, sel_ref, slopes_ref, qt_ref, kt_ref, vt_ref, ck_ref, cv_ref, o_ref,
                        kbuf, vbuf, sem, *, n_pages, past_len):
    b = pl.program_id(0)
    h = pl.program_id(1)
    nb = pl.num_programs(0)
    t_len = qt_ref.shape[-1]
    step = b * ATT_HEADS + h
    slot = step % 2
    per_q = MOBA_TOPK * PAGES_PER_BLOCK
    sel_row = ATT_HEADS * MOBA_TOPK

    def copies(bb, hh, sl, t, kk, p):
        blk = sel_ref[(bb * t_len + t) * sel_row + hh * MOBA_TOPK + kk]
        phys = pt_ref[bb * n_pages + blk * PAGES_PER_BLOCK + p]
        idx = t * per_q + kk * PAGES_PER_BLOCK + p
        return (pltpu.make_async_copy(ck_ref.at[phys, hh], kbuf.at[sl, idx], sem.at[sl, 0]),
                pltpu.make_async_copy(cv_ref.at[phys, hh], vbuf.at[sl, idx], sem.at[sl, 1]))

    def for_all(bb, hh, sl, fn):
        def body(t, carry):
            for kk in range(MOBA_TOPK):
                for p in range(PAGES_PER_BLOCK):
                    for cp in copies(bb, hh, sl, t, kk, p):
                        fn(cp)
            return carry
        lax.fori_loop(0, t_len, body, 0)

    @pl.when(step == 0)
    def _():
        for_all(b, h, slot, lambda cp: cp.start())

    @pl.when(step + 1 < nb * ATT_HEADS)
    def _():
        nxt = step + 1
        for_all(nxt // ATT_HEADS, nxt % ATT_HEADS, 1 - slot, lambda cp: cp.start())

    for_all(b, h, slot, lambda cp: cp.wait())

    slope = slopes_ref[h]
    qt = qt_ref[0, 0]
    kt_new = kt_ref[0, 0]
    vt_new = vt_ref[0, 0]
    key = lax.broadcasted_iota(jnp.int32, (1, PAGE_SIZE), 1)
    new = lax.broadcasted_iota(jnp.int32, (1, t_len), 1)
    out_lane = lax.broadcasted_iota(jnp.int32, (HEAD_DIM, t_len), 1)
    out = jnp.zeros((HEAD_DIM, t_len), F32)
    for t in range(t_len):
        qc = qt[:, t:t + 1]
        scores = []
        for kk in range(MOBA_TOPK):
            blk = sel_ref[(b * t_len + t) * sel_row + h * MOBA_TOPK + kk]
            for p in range(PAGES_PER_BLOCK):
                kt = kbuf[slot, t * per_q + kk * PAGES_PER_BLOCK + p]
                dist = (past_len + t - blk * MOBA_BLOCK - p * PAGE_SIZE - key).astype(F32)
                scores.append(jnp.sum(kt * qc, axis=0, keepdims=True) * SCALE - slope * dist)
        s_new = jnp.sum(kt_new * qc, axis=0, keepdims=True) * SCALE - slope * (t - new).astype(F32)
        s_new = jnp.where(new <= t, s_new, NEG)
        m = jnp.maximum(jnp.max(functools.reduce(jnp.maximum, scores), axis=-1, keepdims=True),
                        jnp.max(s_new, axis=-1, keepdims=True))
        p_new = jnp.exp(s_new - m)
        acc = jnp.zeros((HEAD_DIM, PAGE_SIZE), F32)
        p_sum = jnp.zeros((1, PAGE_SIZE), F32)
        for c, s in enumerate(scores):
            pr = jnp.exp(s - m)
            p_sum = p_sum + pr
            acc = acc + vbuf[slot, t * per_q + c] * pr
        denom = jnp.sum(p_sum, axis=-1, keepdims=True) + jnp.sum(p_new, axis=-1, keepdims=True)
        o = jnp.sum(acc, axis=-1, keepdims=True) + jnp.sum(vt_new * p_new, axis=-1, keepdims=True)
        out = jnp.where(out_lane == t, o * (1.0 / denom), out)
    o_ref[0, 0] = out


def _sample_attn(pt_flat, sel_flat, slopes, qt, kt, vt, ck_t, cv_t, n_pages):
    nheads, nb, _, t_len = qt.shape
    per_step = t_len * MOBA_TOPK * PAGES_PER_BLOCK
    tok = pl.BlockSpec((1, 1, HEAD_DIM, t_len), lambda b, h, pt, sel: (h, b, 0, 0))
    kernel = functools.partial(_sample_attn_kernel, n_pages=n_pages, past_len=n_pages * PAGE_SIZE)
    return pl.pallas_call(
        kernel,
        grid_spec=pltpu.PrefetchScalarGridSpec(
            num_scalar_prefetch=2,
            grid=(nb, nheads),
            in_specs=[pl.BlockSpec(memory_space=pltpu.SMEM), tok, tok, tok,
                      pl.BlockSpec(memory_space=pl.ANY), pl.BlockSpec(memory_space=pl.ANY)],
            out_specs=tok,
            scratch_shapes=[pltpu.VMEM((2, per_step, HEAD_DIM, PAGE_SIZE), F32),
                            pltpu.VMEM((2, per_step, HEAD_DIM, PAGE_SIZE), F32),
                            pltpu.SemaphoreType.DMA((2, 2))]),
        out_shape=jax.ShapeDtypeStruct((nheads, nb, HEAD_DIM, t_len), F32),
        compiler_params=pltpu.CompilerParams(dimension_semantics=("arbitrary", "arbitrary"),
                                             vmem_limit_bytes=VMEM_LIMIT),
        name="sample_attention",
    )(pt_flat, sel_flat, slopes, qt, kt, vt, ck_t, cv_t)


def _prev_ext(cache, t_len):
    nb, w, c = cache.shape
    return jnp.concatenate([cache, jnp.zeros((nb, t_len - w, c), cache.dtype)], axis=1).reshape(nb * t_len, c)


def kernel(x_prompt, x_sample, mem_prompt, cache_k, cache_v, page_table, cache_conv, cache_ffn_conv, cache_mem_k, cache_mem_v, norm_mix, w_in, conv_w, q_norm, k_norm, memq_norm, mem_norm, w_mem_kv, memk_norm, w_o, norm_ffn, w_up, ffn_conv_w, w_down):
    depth = w_in.shape[0]
    assert depth == 1
    nbp, seq, d_model = x_prompt.shape
    nbs, t_len, _ = x_sample.shape
    n_pages = page_table.shape[1]
    assert t_len == SUBLANES and n_pages % PAGES_PER_BLOCK == 0
    nfp = n_pages // PAGES_PER_BLOCK

    slopes = jnp.exp2(-8.0 * jnp.arange(1, ATT_HEADS + 1, dtype=F32) / ATT_HEADS)
    bd = (jnp.kron(jnp.eye(ATT_HEADS, dtype=F32), jnp.ones((HEAD_DIM, HEAD_DIM), F32)) / HEAD_DIM).astype(BF16)
    nm, nf, mn = norm_mix, norm_ffn, mem_norm
    qg = jnp.tile(q_norm, (1, ATT_HEADS))
    kg = jnp.tile(k_norm, (1, ATT_HEADS))
    mqg = jnp.tile(memq_norm, (1, MEM_HEADS))
    mkg = jnp.tile(memk_norm, (1, MEM_HEADS))
    w_in_b = w_in[0].astype(BF16)
    w_o_b = w_o[0].astype(BF16)
    w_up_b = w_up[0].astype(BF16)
    w_down_b = w_down[0].astype(BF16)
    w_mem_b = w_mem_kv[0].astype(BF16)
    cw, fcw = conv_w[0], ffn_conv_w[0]

    mk_t, mv_t = _memkv(mem_prompt, mn, w_mem_b, mkg, bd[:MEM_DIM, :MEM_DIM])
    mem_len = mk_t.shape[-1]
    yc, k_p, v_p, mq, q_aug, k_aug, v_aug, cs_p = _inproj_prompt(slopes, x_prompt, nm, w_in_b, cw, qg, kg, mqg, bd)
    ya = _moba_prompt(q_aug, k_aug, v_aug)
    ym = _memattn(mq, mk_t, mv_t, 2 * MOBA_BLOCK)
    ck_t = jnp.transpose(cache_k[0], (0, 2, 3, 1))
    cv_t = jnp.transpose(cache_v[0], (0, 2, 3, 1))
    y_p, fs_p, page_sum = _post_prompt(x_prompt, yc, ya, ym, w_o_b, nf, w_up_b, fcw, w_down_b, ck_t)

    n_s = nbs * t_len
    xs = x_sample.reshape(n_s, d_model)
    u_s, yc_s, q_s, k_s, v_s, mq_s = _inproj_sample(xs, _prev_ext(cache_conv[0], t_len), nm, w_in[0], cw,
                                                    qg, kg, mqg, bd)
    pt_flat = page_table.reshape(-1)
    sel = _sample_gate(pt_flat, q_s, page_sum, t_len, nfp)
    sel_flat = sel[:, :ATT_HEADS * MOBA_TOPK].reshape(-1)
    heads_t = lambda a: jnp.transpose(a.reshape(nbs, t_len, ATT_HEADS, HEAD_DIM), (2, 0, 3, 1))
    ya_t = _sample_attn(pt_flat, sel_flat, slopes, heads_t(q_s), heads_t(k_s), heads_t(v_s), ck_t, cv_t, n_pages)
    ya_s = jnp.transpose(ya_t, (1, 3, 0, 2)).reshape(n_s, ATT_DIM).astype(BF16)
    cmk_t = jnp.transpose(cache_mem_k[0], (0, 2, 3, 1)).reshape(nbs, MEM_DIM, mem_len)
    cmv_t = jnp.transpose(cache_mem_v[0], (0, 2, 3, 1)).reshape(nbs, MEM_DIM, mem_len)
    ym_s = _memattn(mq_s.reshape(nbs, t_len, MEM_DIM), cmk_t, cmv_t, t_len).reshape(n_s, MEM_DIM)
    y_s, up_s = _post_sample(xs, yc_s, ya_s, ym_s, _prev_ext(cache_ffn_conv[0], t_len), w_o_b, nf, w_up_b,
                             fcw, w_down_b)

    heads = lambda a, nb, t: a.reshape(1, nb, t, ATT_HEADS, HEAD_DIM)
    tail = lambda a, nb, t: a.reshape(nb, t, -1)[None, :, t - 2:]
    mem_out = lambda a: jnp.transpose(a.reshape(nbp, MEM_HEADS, HEAD_DIM, mem_len), (0, 3, 1, 2))[None]
    return (y_p, y_s.reshape(nbs, t_len, d_model),
            heads(k_p, nbp, seq), heads(v_p, nbp, seq), heads(k_s, nbs, t_len), heads(v_s, nbs, t_len),
            cs_p[None, :, SUBLANES - 2:], tail(u_s, nbs, t_len),
            fs_p[None, :, SUBLANES - 2:], tail(up_s, nbs, t_len),
            mem_out(mk_t), mem_out(mv_t))
```
